```python
import math
import jax, jax.numpy as jnp
from jax import lax
import numpy as np

D_MODEL = 1024
BATCH = 2
SEQ = 8192
DEPTH = 4
DEC_BATCH = 128
DEC_SEQ = 1
PAST_LEN = 8192
PAGE_SIZE = 128

D_MIX = D_MODEL
CONV_DIM = D_MIX // 4
POOL_DIM = D_MIX // 4
ATTN_DIM = D_MIX - CONV_DIM - POOL_DIM
HEAD_DIM = 64
CONV_WIDTH = 3
POOL_WINDOWS = (2, 4, 8, 16)
N_POOL_GROUPS = 4
POOL_GROUP_DIM = POOL_DIM // N_POOL_GROUPS
MAX_POOL = 16
N_HEADS = ATTN_DIM // HEAD_DIM
N_KV_HEADS = 2
GQA_GROUP = N_HEADS // N_KV_HEADS
WINDOW = 128
D_FF = 2816
KV_DIM = N_KV_HEADS * HEAD_DIM
IN_SPLITS = (CONV_DIM, CONV_DIM, CONV_DIM, POOL_DIM, ATTN_DIM, KV_DIM, KV_DIM)
D_IN = CONV_DIM * 3 + POOL_DIM + ATTN_DIM + 2 * KV_DIM
N_MOD = 9
ALPHA = (2.0 * DEPTH) ** 0.25
BETA = (8.0 * DEPTH) ** -0.25
LN_EPS = 1e-5
RMS_EPS = 1e-6
NEG_INF = -1e30

kernel_name = "hymba_conv_pool_swa_macaron_decode"


def layer_norm(x, g, b):
    x32 = x.astype(jnp.float32)
    mu = jnp.mean(x32, axis=-1, keepdims=True)
    var = jnp.mean(jnp.square(x32 - mu), axis=-1, keepdims=True)
    return ((x32 - mu) * lax.rsqrt(var + LN_EPS) * g.astype(jnp.float32) + b.astype(jnp.float32)).astype(x.dtype)


def rms_norm(x, g):
    x32 = x.astype(jnp.float32)
    y = x32 * lax.rsqrt(jnp.mean(x32 * x32, axis=-1, keepdims=True) + RMS_EPS)
    return (y * g.astype(jnp.float32)).astype(x.dtype)


def swiglu(h, w_gate, w_up, w_down):
    return (jax.nn.silu(h @ w_gate) * (h @ w_up)) @ w_down


def short_conv(ext, w, t):
    y = w[0] * ext[:, :t]
    for k in range(1, CONV_WIDTH):
        y = y + w[k] * ext[:, k:k + t]
    return y


def multiscale_pool(ext, start_pos, t, w_pool, pool_scale):
    p = MAX_POOL - 1
    x32 = ext.astype(jnp.float32)
    cs = jnp.concatenate([jnp.zeros_like(x32[:, :1]), jnp.cumsum(x32, axis=1)], axis=1)
    pos = start_pos + jnp.arange(t)
    outs = []
    for g, w in enumerate(POOL_WINDOWS):
        lo, hi = g * POOL_GROUP_DIM, (g + 1) * POOL_GROUP_DIM
        win_sum = cs[:, p + 1:p + 1 + t, lo:hi] - cs[:, p + 1 - w:p + 1 - w + t, lo:hi]
        count = jnp.minimum(pos + 1, w).astype(jnp.float32)[None, :, None]
        pooled = (win_sum / count - x32[:, p:, lo:hi]).astype(ext.dtype)
        outs.append(jnp.einsum('btc,cd->btd', pooled, w_pool[g]))
    return jnp.concatenate(outs, axis=-1) * pool_scale


def sink_softmax(s, mask, sinks):
    sink = sinks.astype(jnp.float32).reshape(N_KV_HEADS, GQA_GROUP, 1, 1)
    s = jnp.where(mask, s, NEG_INF)
    m = jnp.maximum(jnp.max(s, axis=-1, keepdims=True), sink)
    p = jnp.exp(s - m)
    return p / (jnp.sum(p, axis=-1, keepdims=True) + jnp.exp(sink - m))


def swa_prompt(q, k, v, sinks):
    b, s = q.shape[0], q.shape[1]
    nb = s // WINDOW
    qb = q.reshape(b, nb, WINDOW, N_KV_HEADS, GQA_GROUP, HEAD_DIM)
    kb = k.reshape(b, nb, WINDOW, N_KV_HEADS, HEAD_DIM)
    vb = v.reshape(b, nb, WINDOW, N_KV_HEADS, HEAD_DIM)
    kk = jnp.concatenate([jnp.concatenate([jnp.zeros_like(kb[:, :1]), kb[:, :-1]], axis=1), kb], axis=2)
    vv = jnp.concatenate([jnp.concatenate([jnp.zeros_like(vb[:, :1]), vb[:, :-1]], axis=1), vb], axis=2)
    scores = jnp.einsum('bnqhgd,bnkhd->bnhgqk', qb, kk).astype(jnp.float32) * (HEAD_DIM ** -0.5)
    blk = jnp.arange(nb)[:, None, None]
    qpos = blk * WINDOW + jnp.arange(WINDOW)[None, :, None]
    kpos = (blk - 1) * WINDOW + jnp.arange(2 * WINDOW)[None, None, :]
    mask = (kpos <= qpos) & (qpos - kpos < WINDOW) & (kpos >= 0)
    probs = sink_softmax(scores, mask[None, :, None, None], sinks)
    out = jnp.einsum('bnhgqk,bnkhd->bnqhgd', probs.astype(vv.dtype), vv)
    return out.reshape(b, s, ATTN_DIM)


def swa_sample(q, k_ext, v_ext, sinks):
    b, t = q.shape[0], q.shape[1]
    L = k_ext.shape[1] - t
    scores = jnp.einsum('bqhgd,bkhd->bhgqk', q, k_ext).astype(jnp.float32) * (HEAD_DIM ** -0.5)
    qpos = jnp.arange(t)[:, None]
    kpos = jnp.arange(L + t)[None, :] - L
    mask = (kpos <= qpos) & (qpos - kpos < WINDOW)
    probs = sink_softmax(scores, mask, sinks)
    out = jnp.einsum('bhgqk,bkhd->bqhgd', probs.astype(v_ext.dtype), v_ext)
    return out.reshape(b, t, ATTN_DIM)


def token_mixer(h, w_in, conv_w, pool_w, pool_scale, sinks, mix_g, w_out,
                conv_prefix, pool_prefix, k_prefix, v_prefix, start_pos):
    b, t, _ = h.shape
    proj = h @ w_in
    split_idx = np.cumsum(IN_SPLITS)[:-1].tolist()
    gb, gc, xin, u, q, k, v = jnp.split(proj, split_idx, axis=-1)
    cv = gc * xin
    conv_ext = jnp.concatenate([conv_prefix.astype(cv.dtype), cv], axis=1)
    y_conv = gb * short_conv(conv_ext, conv_w, t)
    new_conv = conv_ext[:, -(CONV_WIDTH - 1):]
    pool_ext = jnp.concatenate([pool_prefix.astype(u.dtype), u], axis=1)
    y_pool = multiscale_pool(pool_ext, start_pos, t, pool_w, pool_scale)
    new_pool = pool_ext[:, -(MAX_POOL - 1):]
    q = q.reshape(b, t, N_KV_HEADS, GQA_GROUP, HEAD_DIM)
    k = k.reshape(b, t, N_KV_HEADS, HEAD_DIM)
    v = v.reshape(b, t, N_KV_HEADS, HEAD_DIM)
    if k_prefix is None:
        y_attn = swa_prompt(q, k, v, sinks)
        new_k, new_v = k[:, -WINDOW:], v[:, -WINDOW:]
    else:
        buf = k_prefix.shape[1]
        k_ext = jnp.concatenate([k_prefix.astype(k.dtype), k], axis=1)
        v_ext = jnp.concatenate([v_prefix.astype(v.dtype), v], axis=1)
        y_attn = swa_sample(q, k_ext, v_ext, sinks)
        new_k, new_v = k_ext[:, -buf:], v_ext[:, -buf:]
    y = jnp.concatenate([
        rms_norm(y_conv, mix_g[:CONV_DIM]),
        rms_norm(y_pool, mix_g[CONV_DIM:CONV_DIM + POOL_DIM]),
        rms_norm(y_attn, mix_g[CONV_DIM + POOL_DIM:]),
    ], axis=-1)
    return y @ w_out, new_conv, new_pool, new_k, new_v


def run_trunk(x, c, conv_bufs, pool_bufs, k_bufs, v_bufs, start_pos,
              ln_g, ln_b, w_ada, b_ada, ffn1_gate, ffn1_up, ffn1_down,
              w_in, conv_w, pool_w, pool_scale, attn_sinks, mix_norm_g, w_out,
              ffn2_gate, ffn2_up, ffn2_down):
    b = x.shape[0]
    convs, pools, ks, vs = [], [], [], []
    for l in range(DEPTH):
        m = (jax.nn.silu(c) @ w_ada[l] + b_ada[l]).reshape(b, 1, N_MOD, D_MODEL)
        h = x * (1 + m[:, :, 1]) + m[:, :, 0]
        sub = 0.5 * swiglu(h, ffn1_gate[l], ffn1_up[l], ffn1_down[l])
        x = layer_norm(ALPHA * x + m[:, :, 2] * sub, ln_g[l, 0], ln_b[l, 0])
        h = x * (1 + m[:, :, 4]) + m[:, :, 3]
        if conv_bufs is None:
            conv_pre = jnp.zeros((b, CONV_WIDTH - 1, CONV_DIM), x.dtype)
            pool_pre = jnp.zeros((b, MAX_POOL - 1, POOL_DIM), x.dtype)
            k_pre, v_pre = None, None
        else:
            conv_pre, pool_pre, k_pre, v_pre = conv_bufs[l], pool_bufs[l], k_bufs[l], v_bufs[l]
        sub, nc, npool, nk, nv = token_mixer(h, w_in[l], conv_w[l], pool_w[l], pool_scale[l],
                                             attn_sinks[l], mix_norm_g[l], w_out[l],
                                             conv_pre, pool_pre, k_pre, v_pre, start_pos)
        x = layer_norm(ALPHA * x + m[:, :, 5] * sub, ln_g[l, 1], ln_b[l, 1])
        h = x * (1 + m[:, :, 7]) + m[:, :, 6]
        sub = 0.5 * swiglu(h, ffn2_gate[l], ffn2_up[l], ffn2_down[l])
        x = layer_norm(ALPHA * x + m[:, :, 8] * sub, ln_g[l, 2], ln_b[l, 2])
        convs.append(nc)
        pools.append(npool)
        ks.append(nk)
        vs.append(nv)
    return x, jnp.stack(convs), jnp.stack(pools), jnp.stack(ks), jnp.stack(vs)


def setup_inputs(seed: int = 0) -> dict:
    key = jax.random.key(seed)
    ks = jax.random.split(key, 32)
    f32 = jnp.float32
    win_buf = min(WINDOW, PAST_LEN)

    def nrm(k, shape, scale):
        return jax.random.normal(k, shape, f32) * scale

    return {
        'x_prompt': nrm(ks[0], (BATCH, SEQ, D_MODEL), 1.0),
        'x_sample': nrm(ks[1], (DEC_BATCH, DEC_SEQ, D_MODEL), 1.0),
        'state_conv': nrm(ks[2], (DEPTH, DEC_BATCH, CONV_WIDTH - 1, CONV_DIM), 1.0),
        'state_pool': nrm(ks[3], (DEPTH, DEC_BATCH, MAX_POOL - 1, POOL_DIM), 1.0),
        'cache_k_win': nrm(ks[4], (DEPTH, DEC_BATCH, win_buf, N_KV_HEADS, HEAD_DIM), 1.0),
        'cache_v_win': nrm(ks[5], (DEPTH, DEC_BATCH, win_buf, N_KV_HEADS, HEAD_DIM), 1.0),
        'c_prompt': nrm(ks[6], (BATCH, D_MODEL), 1.0),
        'c_sample': nrm(ks[7], (DEC_BATCH, D_MODEL), 1.0),
        'ln_g': 1.0 + nrm(ks[8], (DEPTH, 3, D_MODEL), 0.02),
        'ln_b': nrm(ks[9], (DEPTH, 3, D_MODEL), 0.02),
        'w_ada': nrm(ks[10], (DEPTH, D_MODEL, N_MOD * D_MODEL), 0.5 * D_MODEL ** -0.5),
        'b_ada': nrm(ks[11], (DEPTH, N_MOD * D_MODEL), 0.02),
        'ffn1_gate': nrm(ks[12], (DEPTH, D_MODEL, D_FF), D_MODEL ** -0.5),
        'ffn1_up': nrm(ks[13], (DEPTH, D_MODEL, D_FF), D_MODEL ** -0.5),
        'ffn1_down': nrm(ks[14], (DEPTH, D_FF, D_MODEL), BETA * D_FF ** -0.5),
        'w_in': nrm(ks[15], (DEPTH, D_MODEL, D_IN), D_MODEL ** -0.5),
        'conv_w': nrm(ks[16], (DEPTH, CONV_WIDTH, CONV_DIM), CONV_WIDTH ** -0.5),
        'pool_w': nrm(ks[17], (DEPTH, N_POOL_GROUPS, POOL_GROUP_DIM, POOL_GROUP_DIM), POOL_GROUP_DIM ** -0.5),
        'pool_scale': 1.0 + nrm(ks[18], (DEPTH, POOL_DIM), 0.1),
        'attn_sinks': nrm(ks[19], (DEPTH, N_HEADS), 1.0),
        'mix_norm_g': 1.0 + nrm(ks[20], (DEPTH, D_MIX), 0.1),
        'w_out': nrm(ks[21], (DEPTH, D_MIX, D_MODEL), BETA * D_MIX ** -0.5),
        'ffn2_gate': nrm(ks[22], (DEPTH, D_MODEL, D_FF), D_MODEL ** -0.5),
        'ffn2_up': nrm(ks[23], (DEPTH, D_MODEL, D_FF), D_MODEL ** -0.5),
        'ffn2_down': nrm(ks[24], (DEPTH, D_FF, D_MODEL), BETA * D_FF ** -0.5),
    }


def reference(x_prompt, x_sample, state_conv, state_pool, cache_k_win, cache_v_win,
              c_prompt, c_sample, ln_g, ln_b, w_ada, b_ada, ffn1_gate, ffn1_up, ffn1_down,
              w_in, conv_w, pool_w, pool_scale, attn_sinks, mix_norm_g, w_out,
              ffn2_gate, ffn2_up, ffn2_down):
    y_prompt, p_conv, p_pool, p_k, p_v = run_trunk(
        x_prompt, c_prompt, None, None, None, None, 0,
        ln_g, ln_b, w_ada, b_ada, ffn1_gate, ffn1_up, ffn1_down,
        w_in, conv_w, pool_w, pool_scale, attn_sinks, mix_norm_g, w_out,
        ffn2_gate, ffn2_up, ffn2_down)
    y_sample, s_conv, s_pool, s_k, s_v = run_trunk(
        x_sample, c_sample, state_conv, state_pool, cache_k_win, cache_v_win, PAST_LEN,
        ln_g, ln_b, w_ada, b_ada, ffn1_gate, ffn1_up, ffn1_down,
        w_in, conv_w, pool_w, pool_scale, attn_sinks, mix_norm_g, w_out,
        ffn2_gate, ffn2_up, ffn2_down)
    return (y_prompt, y_sample, p_conv, p_pool, p_k, p_v, s_conv, s_pool, s_k, s_v)
```

```python
import functools

import jax
import jax.numpy as jnp
from jax import lax
from jax.experimental import pallas as pl
from jax.experimental.pallas import tpu as pltpu

D_MODEL = 1024
DEPTH = 4
CONV_DIM = 256
POOL_DIM = 256
ATTN_DIM = 512
HEAD_DIM = 64
CONV_WIDTH = 3
POOL_WINDOWS = (2, 4, 8, 16)
POOL_GROUP_DIM = 64
MAX_POOL = 16
N_HEADS = 8
N_KV_HEADS = 2
GQA_GROUP = 4
WINDOW = 128
D_FF = 2816
KV_DIM = 128
D_IN = 1792
N_MOD = 9
ALPHA = (2.0 * DEPTH) ** 0.25
LN_EPS = 1e-5
RMS_EPS = 1e-6
NEG_INF = -1e30
ATTN_SCALE = HEAD_DIM ** -0.5

OFF_GB, OFF_GC, OFF_XIN, OFF_U, OFF_Q, OFF_K, OFF_V = 0, 256, 512, 768, 1024, 1536, 1664

LANES = 128
HALF_LANES = 64
SUBLANES = 8
VMEM_LIMIT_BYTES = 56 * 1024 * 1024

ADA_ROWS_PAD = 8
FFN_TOKENS = 512
MIX_TOKENS = 512
CONV_PAD = 8
POOL_PAD = 16
SAMPLE_ATTN_BATCH = 16

F32 = jnp.float32
BF16 = jnp.bfloat16


def _const_spec(shape):
    zeros = (0,) * len(shape)
    return pl.BlockSpec(shape, lambda *_: zeros, pipeline_mode=pl.Buffered(1))


def _whole_out_spec(shape):
    zeros = (0,) * len(shape)
    return pl.BlockSpec(shape, lambda *_: zeros)


def _layer_norm(y, g, b):
    mu = jnp.mean(y, axis=-1, keepdims=True)
    d = y - mu
    var = jnp.mean(d * d, axis=-1, keepdims=True)
    return d * lax.rsqrt(var + LN_EPS) * g + b


def _rms_norm(y, g):
    return y * lax.rsqrt(jnp.mean(y * y, axis=-1, keepdims=True) + RMS_EPS) * g


def _low_half(shape):
    return (lax.broadcasted_iota(jnp.int32, shape, len(shape) - 1) % LANES) < HALF_LANES


def _pool_select(s2, s4, s8, s16):
    grp = lax.broadcasted_iota(jnp.int32, s16.shape, 1) // POOL_GROUP_DIM
    return jnp.where(grp == 0, s2, jnp.where(grp == 1, s4, jnp.where(grp == 2, s8, s16)))


def _ada_kernel(c_ref, w_ref, b_ref, op_ref, os_ref):
    c = c_ref[...]
    a = (c * jax.nn.sigmoid(c)).astype(BF16)
    r = jnp.dot(a, w_ref[...].astype(BF16), preferred_element_type=F32) + b_ref[...]
    n_prompt = op_ref.shape[0]
    op_ref[...] = r[:n_prompt]
    os_ref[...] = r[ADA_ROWS_PAD:]


def _ada(c_all, w_ada, b_ada, n_prompt, n_sample):
    rows = c_all.shape[0]
    return pl.pallas_call(
        _ada_kernel,
        grid=(DEPTH, N_MOD),
        in_specs=[
            pl.BlockSpec((rows, D_MODEL), lambda l, j: (0, 0)),
            pl.BlockSpec((None, D_MODEL, D_MODEL), lambda l, j: (l, 0, j)),
            pl.BlockSpec((None, 1, D_MODEL), lambda l, j: (l, 0, j)),
        ],
        out_specs=[
            pl.BlockSpec((None, None, n_prompt, D_MODEL), lambda l, j: (l, j, 0, 0)),
            pl.BlockSpec((None, None, n_sample, D_MODEL), lambda l, j: (l, j, 0, 0)),
        ],
        out_shape=[
            jax.ShapeDtypeStruct((DEPTH, N_MOD, n_prompt, D_MODEL), F32),
            jax.ShapeDtypeStruct((DEPTH, N_MOD, n_sample, D_MODEL), F32),
        ],
        compiler_params=pltpu.CompilerParams(
            dimension_semantics=("arbitrary", "arbitrary"), vmem_limit_bytes=VMEM_LIMIT_BYTES),
        name="ada",
    )(c_all, w_ada, b_ada.reshape(DEPTH, 1, N_MOD * D_MODEL))


def _ffn_kernel(x_ref, m_ref, wg_ref, wu_ref, wd_ref, lng_ref, lnb_ref, o_ref):
    x = x_ref[...]
    shift, scale, gate = m_ref[0], m_ref[1], m_ref[2]
    h = (x * (1.0 + scale) + shift).astype(BF16)
    g = jnp.dot(h, wg_ref[...], preferred_element_type=F32)
    u = jnp.dot(h, wu_ref[...], preferred_element_type=F32)
    a = (g * jax.nn.sigmoid(g) * u).astype(BF16)
    sub = 0.5 * jnp.dot(a, wd_ref[...], preferred_element_type=F32)
    o_ref[...] = _layer_norm(ALPHA * x + gate * sub, lng_ref[...], lnb_ref[...])


def _ffn(x, mod, mod_spec, tokens, wg, wu, wd, ln_g, ln_b, name):
    n = x.shape[0]
    return pl.pallas_call(
        _ffn_kernel,
        grid=(n // tokens,),
        in_specs=[
            pl.BlockSpec((tokens, D_MODEL), lambda i: (i, 0)),
            mod_spec,
            _const_spec((D_MODEL, D_FF)),
            _const_spec((D_MODEL, D_FF)),
            _const_spec((D_FF, D_MODEL)),
            _const_spec((1, D_MODEL)),
            _const_spec((1, D_MODEL)),
        ],
        out_specs=pl.BlockSpec((tokens, D_MODEL), lambda i: (i, 0)),
        out_shape=jax.ShapeDtypeStruct((n, D_MODEL), F32),
        compiler_params=pltpu.CompilerParams(
            dimension_semantics=("arbitrary",), vmem_limit_bytes=VMEM_LIMIT_BYTES),
        name=name,
    )(x, mod, wg, wu, wd, ln_g, ln_b)


def _prompt_mixer_kernel(tiles_per_seq,
                         sink_ref, x_ref, m_ref, win_ref, convw_ref, poolw_ref, pscale_ref,
                         mixg_ref, wout_ref, lng_ref, lnb_ref,
                         o_ref, oconv_ref, opool_ref, ok_ref, ov_ref,
                         cv_ext, u_ext, kd0, kd1, vd0, vd1, yattn):
    tm = x_ref.shape[0]
    tile = pl.program_id(0) % tiles_per_seq

    @pl.when(tile == 0)
    def _():
        cv_ext[0:CONV_PAD, :] = jnp.zeros((CONV_PAD, CONV_DIM), F32)
        u_ext[0:POOL_PAD, :] = jnp.zeros((POOL_PAD, POOL_DIM), F32)
        for ref in (kd0, kd1, vd0, vd1):
            ref[0:WINDOW, :] = jnp.zeros((WINDOW, KV_DIM), BF16)

    x = x_ref[...]
    shift, scale, gate = m_ref[0], m_ref[1], m_ref[2]
    h = (x * (1.0 + scale) + shift).astype(BF16)
    proj = jnp.dot(h, win_ref[...], preferred_element_type=F32)

    cv = proj[:, OFF_GC:OFF_GC + CONV_DIM] * proj[:, OFF_XIN:OFF_XIN + CONV_DIM]
    cv_ext[CONV_PAD:CONV_PAD + tm, :] = cv
    conv = convw_ref[CONV_WIDTH - 1:CONV_WIDTH, :] * cv
    for kk in range(CONV_WIDTH - 1):
        back = CONV_WIDTH - 1 - kk
        conv = conv + convw_ref[kk:kk + 1, :] * cv_ext[CONV_PAD - back:CONV_PAD - back + tm, :]
    y_conv = proj[:, OFF_GB:OFF_GB + CONV_DIM] * conv
    oconv_ref[...] = cv_ext[CONV_PAD + tm - (CONV_WIDTH - 1):CONV_PAD + tm, :]
    cv_ext[0:CONV_PAD, :] = cv_ext[tm:tm + CONV_PAD, :]

    u = proj[:, OFF_U:OFF_U + POOL_DIM]
    u_ext[POOL_PAD:POOL_PAD + tm, :] = u
    acc = u
    sums = {}
    for back in range(1, MAX_POOL):
        acc = acc + u_ext[POOL_PAD - back:POOL_PAD - back + tm, :]
        if back + 1 in POOL_WINDOWS:
            sums[back + 1] = acc
    pos1 = tile * tm + lax.broadcasted_iota(jnp.int32, (tm, 1), 0) + 1
    counts = [jnp.minimum(pos1, w).astype(F32) for w in POOL_WINDOWS]
    win_sum = _pool_select(*[sums[w] for w in POOL_WINDOWS])
    count = _pool_select(*[jnp.broadcast_to(c, (tm, POOL_DIM)) for c in counts])
    pooled = (win_sum / count - u).astype(BF16)
    y_pool = jnp.dot(pooled, poolw_ref[...], preferred_element_type=F32) * pscale_ref[...]
    opool_ref[...] = u_ext[POOL_PAD + tm - (MAX_POOL - 1):POOL_PAD + tm, :]
    u_ext[0:POOL_PAD, :] = u_ext[tm:tm + POOL_PAD, :]

    k = proj[:, OFF_K:OFF_K + KV_DIM]
    v = proj[:, OFF_V:OFF_V + KV_DIM]
    ok_ref[...] = k[tm - WINDOW:, :]
    ov_ref[...] = v[tm - WINDOW:, :]
    low = _low_half((tm, KV_DIM))
    k_sw = pltpu.roll(k, HALF_LANES, 1)
    v_sw = pltpu.roll(v, HALF_LANES, 1)
    kd0[WINDOW:WINDOW + tm, :] = jnp.where(low, k, k_sw).astype(BF16)
    kd1[WINDOW:WINDOW + tm, :] = jnp.where(low, k_sw, k).astype(BF16)
    vd0[WINDOW:WINDOW + tm, :] = jnp.where(low, v, v_sw).astype(BF16)
    vd1[WINDOW:WINDOW + tm, :] = jnp.where(low, v_sw, v).astype(BF16)

    rows = GQA_GROUP * WINDOW
    r = lax.broadcasted_iota(jnp.int32, (rows, 2 * WINDOW), 0)
    qidx = r % WINDOW
    kidx = lax.broadcasted_iota(jnp.int32, (rows, 2 * WINDOW), 1)
    band = (kidx > qidx) & (kidx <= qidx + WINDOW)
    first_key = jnp.where(tile == 0, WINDOW, 0)
    head_row = lax.broadcasted_iota(jnp.int32, (rows, 1), 0) // WINDOW
    low_q = _low_half((WINDOW, LANES))
    for g, (kd, vd) in enumerate(((kd0, vd0), (kd1, vd1))):
        sink = jnp.zeros((rows, 1), F32)
        for hh in range(GQA_GROUP):
            sink = jnp.where(head_row == hh, sink_ref[g * GQA_GROUP + hh], sink)
        for j in range(tm // WINDOW):
            mask = (band & (kidx >= first_key)) if j == 0 else band
            q0 = OFF_Q + g * GQA_GROUP * HEAD_DIM
            qa = proj[j * WINDOW:(j + 1) * WINDOW, q0:q0 + LANES]
            qb = proj[j * WINDOW:(j + 1) * WINDOW, q0 + LANES:q0 + 2 * LANES]
            q4 = jnp.concatenate([jnp.where(low_q, qa, 0.0), jnp.where(low_q, 0.0, qa),
                                  jnp.where(low_q, qb, 0.0), jnp.where(low_q, 0.0, qb)],
                                 axis=0).astype(BF16)
            kk = kd[j * WINDOW:(j + 2) * WINDOW, :]
            vv = vd[j * WINDOW:(j + 2) * WINDOW, :]
            s = lax.dot_general(q4, kk, (((1,), (1,)), ((), ())), preferred_element_type=F32)
            s = jnp.where(mask, s * ATTN_SCALE, NEG_INF)
            mx = jnp.maximum(jnp.max(s, axis=-1, keepdims=True), sink)
            p = jnp.exp(s - mx)
            den = jnp.sum(p, axis=-1, keepdims=True) + jnp.exp(sink - mx)
            o = jnp.dot(p.astype(BF16), vv, preferred_element_type=F32) / den
            ya = jnp.where(low_q, o[0:WINDOW], o[WINDOW:2 * WINDOW])
            yb = jnp.where(low_q, o[2 * WINDOW:3 * WINDOW], o[3 * WINDOW:4 * WINDOW])
            c0 = g * GQA_GROUP * HEAD_DIM
            yattn[j * WINDOW:(j + 1) * WINDOW, c0:c0 + LANES] = ya
            yattn[j * WINDOW:(j + 1) * WINDOW, c0 + LANES:c0 + 2 * LANES] = yb
    for ref in (kd0, kd1, vd0, vd1):
        ref[0:WINDOW, :] = ref[tm:tm + WINDOW, :]

    mixg = mixg_ref[...]
    y = jnp.concatenate([
        _rms_norm(y_conv, mixg[:, 0:CONV_DIM]),
        _rms_norm(y_pool, mixg[:, CONV_DIM:CONV_DIM + POOL_DIM]),
        _rms_norm(yattn[...], mixg[:, CONV_DIM + POOL_DIM:]),
    ], axis=-1).astype(BF16)
    sub = jnp.dot(y, wout_ref[...], preferred_element_type=F32)
    o_ref[...] = _layer_norm(ALPHA * x + gate * sub, lng_ref[...], lnb_ref[...])


def _prompt_mixer(x, mod, l, sinks, w_in, conv_w, pool_wbd, pool_scale, mix_g, w_out,
                  ln_g, ln_b, batch, seq):
    tm = MIX_TOKENS
    tiles_per_seq = seq // tm
    n = x.shape[0]
    return pl.pallas_call(
        functools.partial(_prompt_mixer_kernel, tiles_per_seq),
        grid=(n // tm,),
        in_specs=[
            pl.BlockSpec(memory_space=pltpu.SMEM),
            pl.BlockSpec((tm, D_MODEL), lambda i: (i, 0)),
            pl.BlockSpec((None, 3, None, 1, D_MODEL), lambda i: (l, 1, i // tiles_per_seq, 0, 0)),
            _const_spec((D_MODEL, D_IN)),
            _const_spec((CONV_WIDTH, CONV_DIM)),
            _const_spec((POOL_DIM, POOL_DIM)),
            _const_spec((1, POOL_DIM)),
            _const_spec((1, D_MODEL)),
            _const_spec((D_MODEL, D_MODEL)),
            _const_spec((1, D_MODEL)),
            _const_spec((1, D_MODEL)),
        ],
        out_specs=[
            pl.BlockSpec((tm, D_MODEL), lambda i: (i, 0)),
            pl.BlockSpec((None, CONV_WIDTH - 1, CONV_DIM), lambda i: (i // tiles_per_seq, 0, 0)),
            pl.BlockSpec((None, MAX_POOL - 1, POOL_DIM), lambda i: (i // tiles_per_seq, 0, 0)),
            pl.BlockSpec((None, WINDOW, KV_DIM), lambda i: (i // tiles_per_seq, 0, 0)),
            pl.BlockSpec((None, WINDOW, KV_DIM), lambda i: (i // tiles_per_seq, 0, 0)),
        ],
        out_shape=[
            jax.ShapeDtypeStruct((n, D_MODEL), F32),
            jax.ShapeDtypeStruct((batch, CONV_WIDTH - 1, CONV_DIM), F32),
            jax.ShapeDtypeStruct((batch, MAX_POOL - 1, POOL_DIM), F32),
            jax.ShapeDtypeStruct((batch, WINDOW, KV_DIM), F32),
            jax.ShapeDtypeStruct((batch, WINDOW, KV_DIM), F32),
        ],
        scratch_shapes=[
            pltpu.VMEM((CONV_PAD + tm, CONV_DIM), F32),
            pltpu.VMEM((POOL_PAD + tm, POOL_DIM), F32),
            pltpu.VMEM((WINDOW + tm, KV_DIM), BF16),
            pltpu.VMEM((WINDOW + tm, KV_DIM), BF16),
            pltpu.VMEM((WINDOW + tm, KV_DIM), BF16),
            pltpu.VMEM((WINDOW + tm, KV_DIM), BF16),
            pltpu.VMEM((tm, ATTN_DIM), F32),
        ],
        compiler_params=pltpu.CompilerParams(
            dimension_semantics=("arbitrary",), vmem_limit_bytes=VMEM_LIMIT_BYTES),
        name="prompt_mixer",
    )(sinks, x, mod, w_in, conv_w, pool_wbd, pool_scale, mix_g, w_out, ln_g, ln_b)


def _sample_proj_kernel(x_ref, m_ref, win_ref, convw_ref, poolw_ref, pscale_ref, mixg_ref,
                        sconv_ref, spool_ref,
                        qm_ref, knew_ref, vnew_ref, ycp_ref, nconv_ref, npool_ref):
    x = x_ref[...]
    shift, scale = m_ref[0], m_ref[1]
    h = (x * (1.0 + scale) + shift).astype(BF16)
    proj = jnp.dot(h, win_ref[...], preferred_element_type=F32)

    cv = proj[:, OFF_GC:OFF_GC + CONV_DIM] * proj[:, OFF_XIN:OFF_XIN + CONV_DIM]
    conv = convw_ref[CONV_WIDTH - 1:CONV_WIDTH, :] * cv
    for kk in range(CONV_WIDTH - 1):
        conv = conv + convw_ref[kk:kk + 1, :] * sconv_ref[:, kk * CONV_DIM:(kk + 1) * CONV_DIM]
    y_conv = proj[:, OFF_GB:OFF_GB + CONV_DIM] * conv
    keep = (CONV_WIDTH - 2) * CONV_DIM
    nconv_ref[:, 0:keep] = sconv_ref[:, CONV_DIM:CONV_DIM + keep]
    nconv_ref[:, keep:keep + CONV_DIM] = cv

    u = proj[:, OFF_U:OFF_U + POOL_DIM]
    acc = u
    sums = {}
    for back in range(1, MAX_POOL):
        t = MAX_POOL - 1 - back
        acc = acc + spool_ref[:, t * POOL_DIM:(t + 1) * POOL_DIM]
        if back + 1 in POOL_WINDOWS:
            sums[back + 1] = acc
    win_sum = _pool_select(*[sums[w] for w in POOL_WINDOWS])
    count = _pool_select(*[jnp.full(u.shape, float(w), F32) for w in POOL_WINDOWS])
    pooled = (win_sum / count - u).astype(BF16)
    y_pool = jnp.dot(pooled, poolw_ref[...], preferred_element_type=F32) * pscale_ref[...]
    keep = (MAX_POOL - 2) * POOL_DIM
    npool_ref[:, 0:keep] = spool_ref[:, POOL_DIM:POOL_DIM + keep]
    npool_ref[:, keep:keep + POOL_DIM] = u

    mixg = mixg_ref[...]
    ycp_ref[:, 0:CONV_DIM] = _rms_norm(y_conv, mixg[:, 0:CONV_DIM])
    ycp_ref[:, CONV_DIM:CONV_DIM + POOL_DIM] = _rms_norm(y_pool, mixg[:, CONV_DIM:CONV_DIM + POOL_DIM])

    knew_ref[...] = proj[:, OFF_K:OFF_K + KV_DIM]
    vnew_ref[...] = proj[:, OFF_V:OFF_V + KV_DIM]
    low = _low_half((x.shape[0], LANES))
    for pair in range(N_HEADS // 2):
        g = (2 * pair) // GQA_GROUP
        qp = proj[:, OFF_Q + pair * LANES:OFF_Q + (pair + 1) * LANES]
        qs = pltpu.roll(qp, HALF_LANES, 1)
        if g == 0:
            first, second = jnp.where(low, qp, 0.0), jnp.where(low, qs, 0.0)
        else:
            first, second = jnp.where(low, 0.0, qs), jnp.where(low, 0.0, qp)
        qm_ref[:, (2 * pair) * LANES:(2 * pair + 1) * LANES] = first
        qm_ref[:, (2 * pair + 1) * LANES:(2 * pair + 2) * LANES] = second


def _sample_proj(x, mod_s, l, w_in, conv_w, pool_wbd, pool_scale, mix_g, sconv, spool):
    n = x.shape[0]
    return pl.pallas_call(
        _sample_proj_kernel,
        grid=(1,),
        in_specs=[
            _const_spec((n, D_MODEL)),
            pl.BlockSpec((None, 3, n, D_MODEL), lambda i: (l, 1, 0, 0)),
            _const_spec((D_MODEL, D_IN)),
            _const_spec((CONV_WIDTH, CONV_DIM)),
            _const_spec((POOL_DIM, POOL_DIM)),
            _const_spec((1, POOL_DIM)),
            _const_spec((1, D_MODEL)),
            _const_spec((n, (CONV_WIDTH - 1) * CONV_DIM)),
            _const_spec((n, (MAX_POOL - 1) * POOL_DIM)),
        ],
        out_specs=[
            _whole_out_spec((n, N_HEADS * LANES)),
            _whole_out_spec((n, KV_DIM)),
            _whole_out_spec((n, KV_DIM)),
            _whole_out_spec((n, CONV_DIM + POOL_DIM)),
            _whole_out_spec((n, (CONV_WIDTH - 1) * CONV_DIM)),
            _whole_out_spec((n, (MAX_POOL - 1) * POOL_DIM)),
        ],
        out_shape=[
            jax.ShapeDtypeStruct((n, N_HEADS * LANES), F32),
            jax.ShapeDtypeStruct((n, KV_DIM), F32),
            jax.ShapeDtypeStruct((n, KV_DIM), F32),
            jax.ShapeDtypeStruct((n, CONV_DIM + POOL_DIM), F32),
            jax.ShapeDtypeStruct((n, (CONV_WIDTH - 1) * CONV_DIM), F32),
            jax.ShapeDtypeStruct((n, (MAX_POOL - 1) * POOL_DIM), F32),
        ],
        compiler_params=pltpu.CompilerParams(
            dimension_semantics=("arbitrary",), vmem_limit_bytes=VMEM_LIMIT_BYTES),
        name="sample_proj",
    )(x, mod_s, w_in, conv_w, pool_wbd, pool_scale, mix_g, sconv, spool)


def _sample_attn_kernel(qm_ref, knew_ref, vnew_ref, ck_ref, cv_ref, sink_ref,
                        nk_ref, nv_ref, o_ref):
    nk_ref[:, 0:WINDOW - 1, :] = ck_ref[:, 1:WINDOW, :]
    nk_ref[:, WINDOW - 1:WINDOW, :] = knew_ref[...]
    nv_ref[:, 0:WINDOW - 1, :] = cv_ref[:, 1:WINDOW, :]
    nv_ref[:, WINDOW - 1:WINDOW, :] = vnew_ref[...]
    q = qm_ref[...].astype(BF16)
    s = jnp.einsum('bhd,bkd->bhk', q, nk_ref[...].astype(BF16),
                   preferred_element_type=F32) * ATTN_SCALE
    sink = sink_ref[...][None]
    mx = jnp.maximum(jnp.max(s, axis=-1, keepdims=True), sink)
    p = jnp.exp(s - mx)
    den = jnp.sum(p, axis=-1, keepdims=True) + jnp.exp(sink - mx)
    o = jnp.einsum('bhk,bkd->bhd', p.astype(BF16), nv_ref[...].astype(BF16),
                   preferred_element_type=F32)
    o_ref[...] = o / den


def _sample_attn(qm, knew, vnew, cache_k, cache_v, l, sinks):
    n = qm.shape[0]
    bb = SAMPLE_ATTN_BATCH
    row = pl.BlockSpec((bb, 1, KV_DIM), lambda i: (i, 0, 0))
    cache = pl.BlockSpec((None, bb, WINDOW, KV_DIM), lambda i: (l, i, 0, 0))
    win = pl.BlockSpec((bb, WINDOW, KV_DIM), lambda i: (i, 0, 0))
    heads = pl.BlockSpec((bb, N_HEADS, LANES), lambda i: (i, 0, 0))
    return pl.pallas_call(
        _sample_attn_kernel,
        grid=(n // bb,),
        in_specs=[heads, row, row, cache, cache, _const_spec((N_HEADS, 1))],
        out_specs=[win, win, heads],
        out_shape=[
            jax.ShapeDtypeStruct((n, WINDOW, KV_DIM), F32),
            jax.ShapeDtypeStruct((n, WINDOW, KV_DIM), F32),
            jax.ShapeDtypeStruct((n, N_HEADS, LANES), F32),
        ],
        compiler_params=pltpu.CompilerParams(
            dimension_semantics=("arbitrary",), vmem_limit_bytes=VMEM_LIMIT_BYTES),
        name="sample_attn",
    )(qm, knew, vnew, cache_k, cache_v, sinks)


def _sample_out_kernel(x_ref, m_ref, ycp_ref, oh_ref, mixg_ref, wout_ref, lng_ref, lnb_ref, o_ref):
    x = x_ref[...]
    gate = m_ref[2]
    low = _low_half((x.shape[0], LANES))
    pairs = []
    for pair in range(N_HEADS // 2):
        g = (2 * pair) // GQA_GROUP
        oa = oh_ref[:, (2 * pair) * LANES:(2 * pair + 1) * LANES]
        ob = oh_ref[:, (2 * pair + 1) * LANES:(2 * pair + 2) * LANES]
        if g == 0:
            pairs.append(jnp.where(low, oa, pltpu.roll(ob, HALF_LANES, 1)))
        else:
            pairs.append(jnp.where(low, pltpu.roll(oa, HALF_LANES, 1), ob))
    y_attn = jnp.concatenate(pairs, axis=-1)
    mixg = mixg_ref[...]
    y = jnp.concatenate([ycp_ref[...], _rms_norm(y_attn, mixg[:, CONV_DIM + POOL_DIM:])],
                        axis=-1).astype(BF16)
    sub = jnp.dot(y, wout_ref[...], preferred_element_type=F32)
    o_ref[...] = _layer_norm(ALPHA * x + gate * sub, lng_ref[...], lnb_ref[...])


def _sample_out(x, mod_s, l, ycp, oh, mix_g, w_out, ln_g, ln_b):
    n = x.shape[0]
    return pl.pallas_call(
        _sample_out_kernel,
        grid=(1,),
        in_specs=[
            _const_spec((n, D_MODEL)),
            pl.BlockSpec((None, 3, n, D_MODEL), lambda i: (l, 1, 0, 0)),
            _const_spec((n, CONV_DIM + POOL_DIM)),
            _const_spec((n, N_HEADS * LANES)),
            _const_spec((1, D_MODEL)),
            _const_spec((D_MODEL, D_MODEL)),
            _const_spec((1, D_MODEL)),
            _const_spec((1, D_MODEL)),
        ],
        out_specs=_whole_out_spec((n, D_MODEL)),
        out_shape=jax.ShapeDtypeStruct((n, D_MODEL), F32),
        compiler_params=pltpu.CompilerParams(
            dimension_semantics=("arbitrary",), vmem_limit_bytes=VMEM_LIMIT_BYTES),
        name="sample_out",
    )(x, mod_s, ycp, oh, mix_g, w_out, ln_g, ln_b)


def _block_diag(w):
    n, d, _ = w.shape
    eye = jnp.eye(n, dtype=w.dtype)
    return (eye[:, None, :, None] * w[:, :, None, :]).reshape(n * d, n * d)


def kernel(x_prompt, x_sample, state_conv, state_pool, cache_k_win, cache_v_win, c_prompt, c_sample, ln_g, ln_b, w_ada, b_ada, ffn1_gate, ffn1_up, ffn1_down, w_in, conv_w, pool_w, pool_scale, attn_sinks, mix_norm_g, w_out, ffn2_gate, ffn2_up, ffn2_down):
    batch, seq, _ = x_prompt.shape
    n_sample = x_sample.shape[0]
    win_buf = cache_k_win.shape[2]
    assert x_sample.shape[1] == 1 and win_buf == WINDOW
    assert seq % MIX_TOKENS == 0 and (batch * seq) % FFN_TOKENS == 0 and seq % FFN_TOKENS == 0
    assert n_sample % SAMPLE_ATTN_BATCH == 0 and batch <= ADA_ROWS_PAD

    c_all = jnp.concatenate(
        [c_prompt, jnp.zeros((ADA_ROWS_PAD - batch, D_MODEL), F32), c_sample], axis=0)
    mod_p, mod_s = _ada(c_all, w_ada, b_ada, batch, n_sample)
    mod_p = mod_p.reshape(DEPTH, N_MOD, batch, 1, D_MODEL)

    xp = x_prompt.reshape(batch * seq, D_MODEL)
    xs = x_sample.reshape(n_sample, D_MODEL)
    sconv = state_conv.reshape(DEPTH, n_sample, (CONV_WIDTH - 1) * CONV_DIM)
    spool = state_pool.reshape(DEPTH, n_sample, (MAX_POOL - 1) * POOL_DIM)
    cache_k = cache_k_win.reshape(DEPTH, n_sample, WINDOW, KV_DIM)
    cache_v = cache_v_win.reshape(DEPTH, n_sample, WINDOW, KV_DIM)

    ffn_tiles_per_seq = seq // FFN_TOKENS
    outs = {k: [] for k in ("pc", "pp", "pk", "pv", "sc", "sp", "sk", "sv")}
    for l in range(DEPTH):
        w1 = [w[l].astype(BF16) for w in (ffn1_gate, ffn1_up, ffn1_down)]
        w2 = [w[l].astype(BF16) for w in (ffn2_gate, ffn2_up, ffn2_down)]
        w_in_l = w_in[l].astype(BF16)
        w_out_l = w_out[l].astype(BF16)
        pool_wbd = _block_diag(pool_w[l]).astype(BF16)
        pscale = pool_scale[l].reshape(1, POOL_DIM)
        mixg = mix_norm_g[l].reshape(1, D_MODEL)
        lng = [ln_g[l, s].reshape(1, D_MODEL) for s in range(3)]
        lnb = [ln_b[l, s].reshape(1, D_MODEL) for s in range(3)]

        def p_spec(sub, l=l):
            return pl.BlockSpec((None, 3, None, 1, D_MODEL),
                                lambda i: (l, sub, i // ffn_tiles_per_seq, 0, 0))

        def s_spec(sub, l=l):
            return pl.BlockSpec((None, 3, n_sample, D_MODEL), lambda i: (l, sub, 0, 0))

        xp = _ffn(xp, mod_p, p_spec(0), FFN_TOKENS, *w1, lng[0], lnb[0], "ffn_prompt")
        xp, pc, pp, pk, pv = _prompt_mixer(
            xp, mod_p, l, attn_sinks[l], w_in_l, conv_w[l], pool_wbd, pscale, mixg, w_out_l,
            lng[1], lnb[1], batch, seq)
        xp = _ffn(xp, mod_p, p_spec(2), FFN_TOKENS, *w2, lng[2], lnb[2], "ffn_prompt")

        xs = _ffn(xs, mod_s, s_spec(0), n_sample, *w1, lng[0], lnb[0], "ffn_sample")
        qm, knew, vnew, ycp, nconv, npool = _sample_proj(
            xs, mod_s, l, w_in_l, conv_w[l], pool_wbd, pscale, mixg, sconv[l], spool[l])
        nk, nv, oh = _sample_attn(
            qm.reshape(n_sample, N_HEADS, LANES), knew.reshape(n_sample, 1, KV_DIM),
            vnew.reshape(n_sample, 1, KV_DIM), cache_k, cache_v, l,
            attn_sinks[l].reshape(N_HEADS, 1))
        xs = _sample_out(xs, mod_s, l, ycp, oh.reshape(n_sample, N_HEADS * LANES), mixg, w_out_l,
                         lng[1], lnb[1])
        xs = _ffn(xs, mod_s, s_spec(2), n_sample, *w2, lng[2], lnb[2], "ffn_sample")

        for key, val in zip(("pc", "pp", "pk", "pv", "sc", "sp", "sk", "sv"),
                            (pc, pp, pk, pv, nconv, npool, nk, nv)):
            outs[key].append(val)

    kv_shape = (DEPTH, -1, WINDOW, N_KV_HEADS, HEAD_DIM)
    return (
        xp.reshape(batch, seq, D_MODEL),
        xs.reshape(n_sample, 1, D_MODEL),
        jnp.stack(outs["pc"]),
        jnp.stack(outs["pp"]),
        jnp.stack(outs["pk"]).reshape(kv_shape),
        jnp.stack(outs["pv"]).reshape(kv_shape),
        jnp.stack(outs["sc"]).reshape(DEPTH, n_sample, CONV_WIDTH - 1, CONV_DIM),
        jnp.stack(outs["sp"]).reshape(DEPTH, n_sample, MAX_POOL - 1, POOL_DIM),
        jnp.stack(outs["sk"]).reshape(kv_shape),
        jnp.stack(outs["sv"]).reshape(kv_shape),
    )
```

```python
import functools

import jax
import jax.numpy as jnp
from jax import lax
from jax.experimental import pallas as pl
from jax.experimental.pallas import tpu as pltpu

D_MODEL = 1024
DEPTH = 4
CONV_DIM = 256
POOL_DIM = 256
ATTN_DIM = 512
HEAD_DIM = 64
CONV_WIDTH = 3
POOL_WINDOWS = (2, 4, 8, 16)
POOL_GROUP_DIM = 64
MAX_POOL = 16
N_HEADS = 8
N_KV_HEADS = 2
GQA_GROUP = 4
WINDOW = 128
D_FF = 2816
KV_DIM = 128
D_IN = 1792
N_MOD = 9
ALPHA = (2.0 * DEPTH) ** 0.25
LN_EPS = 1e-5
RMS_EPS = 1e-6
NEG_INF = -1e30
ATTN_SCALE = HEAD_DIM ** -0.5

OFF_GB, OFF_GC, OFF_XIN, OFF_U, OFF_Q, OFF_K, OFF_V = 0, 256, 512, 768, 1024, 1536, 1664

LANES = 128
HALF_LANES = 64
SUBLANES = 8
VMEM_LIMIT_BYTES = 56 * 1024 * 1024

ADA_ROWS_PAD = 8
FFN_TOKENS = 512
FFN_CHUNK = 256
FFN_CHUNKS = D_FF // FFN_CHUNK
MIX_TOKENS = 512
CONV_PAD = 8
POOL_PAD = 16
SAMPLE_ATTN_BATCH = 16

F32 = jnp.float32
BF16 = jnp.bfloat16


def _const_spec(shape):
    zeros = (0,) * len(shape)
    return pl.BlockSpec(shape, lambda *_: zeros, pipeline_mode=pl.Buffered(1))


def _layer_spec(shape, l):
    zeros = (0,) * len(shape)
    return pl.BlockSpec((None,) + tuple(shape), lambda *_: (l,) + zeros,
                        pipeline_mode=pl.Buffered(1))


def _whole_out_spec(shape):
    zeros = (0,) * len(shape)
    return pl.BlockSpec(shape, lambda *_: zeros)


def _layer_norm(y, g, b):
    mu = jnp.mean(y, axis=-1, keepdims=True)
    d = y - mu
    var = jnp.mean(d * d, axis=-1, keepdims=True)
    return d * lax.rsqrt(var + LN_EPS) * g + b


def _rms_norm(y, g):
    return y * lax.rsqrt(jnp.mean(y * y, axis=-1, keepdims=True) + RMS_EPS) * g


def _low_half(shape):
    return (lax.broadcasted_iota(jnp.int32, shape, len(shape) - 1) % LANES) < HALF_LANES


def _pool_select(s2, s4, s8, s16):
    grp = lax.broadcasted_iota(jnp.int32, s16.shape, 1) // POOL_GROUP_DIM
    return jnp.where(grp == 0, s2, jnp.where(grp == 1, s4, jnp.where(grp == 2, s8, s16)))


def _ada_kernel(c_ref, w_ref, b_ref, op_ref, os_ref):
    c = c_ref[...]
    a = (c * jax.nn.sigmoid(c)).astype(BF16)
    r = jnp.dot(a, w_ref[...].astype(BF16), preferred_element_type=F32) + b_ref[...]
    n_prompt = op_ref.shape[0]
    op_ref[...] = r[:n_prompt]
    os_ref[...] = r[ADA_ROWS_PAD:]


def _ada(c_all, w_ada, b_ada, n_prompt, n_sample):
    rows = c_all.shape[0]
    return pl.pallas_call(
        _ada_kernel,
        grid=(DEPTH, N_MOD),
        in_specs=[
            pl.BlockSpec((rows, D_MODEL), lambda l, j: (0, 0)),
            pl.BlockSpec((None, D_MODEL, D_MODEL), lambda l, j: (l, 0, j)),
            pl.BlockSpec((None, 1, D_MODEL), lambda l, j: (l, 0, j)),
        ],
        out_specs=[
            pl.BlockSpec((None, None, n_prompt, D_MODEL), lambda l, j: (l, j, 0, 0)),
            pl.BlockSpec((None, None, n_sample, D_MODEL), lambda l, j: (l, j, 0, 0)),
        ],
        out_shape=[
            jax.ShapeDtypeStruct((DEPTH, N_MOD, n_prompt, D_MODEL), F32),
            jax.ShapeDtypeStruct((DEPTH, N_MOD, n_sample, D_MODEL), F32),
        ],
        compiler_params=pltpu.CompilerParams(
            dimension_semantics=("arbitrary", "arbitrary"), vmem_limit_bytes=VMEM_LIMIT_BYTES),
        name="ada",
    )(c_all, w_ada, b_ada.reshape(DEPTH, 1, N_MOD * D_MODEL))


def _ffn_kernel(n_tiles,
                xp_ref, mp_ref, xs_ref, ms_ref, wg_ref, wu_ref, wd_ref, lng_ref, lnb_ref,
                op_ref, os_ref, wg_bf, wu_bf, wd_bf):
    i = pl.program_id(0)

    for c in range(FFN_CHUNKS):
        @pl.when(i == c)
        def _(c=c):
            cols = slice(c * FFN_CHUNK, (c + 1) * FFN_CHUNK)
            wg_bf[:, cols] = wg_ref[...].astype(BF16)
            wu_bf[:, cols] = wu_ref[...].astype(BF16)
            wd_bf[cols, :] = wd_ref[...].astype(BF16)

    def ffn(x_ref, m_ref, o_ref):
        x = x_ref[...]
        shift, scale, gate = m_ref[0], m_ref[1], m_ref[2]
        h = (x * (1.0 + scale) + shift).astype(BF16)
        g = jnp.dot(h, wg_bf[...], preferred_element_type=F32)
        u = jnp.dot(h, wu_bf[...], preferred_element_type=F32)
        a = (g * jax.nn.sigmoid(g) * u).astype(BF16)
        sub = 0.5 * jnp.dot(a, wd_bf[...], preferred_element_type=F32)
        o_ref[...] = _layer_norm(ALPHA * x + gate * sub, lng_ref[...], lnb_ref[...])

    @pl.when((i >= FFN_CHUNKS) & (i < FFN_CHUNKS + n_tiles))
    def _():
        ffn(xp_ref, mp_ref, op_ref)

    @pl.when(i == FFN_CHUNKS + n_tiles)
    def _():
        ffn(xs_ref, ms_ref, os_ref)


def _ffn(xp, xs, mod_p, mod_s, l, sub, tiles_per_seq, wg, wu, wd, ln_g, ln_b):
    n, n_sample = xp.shape[0], xs.shape[0]
    tm = FFN_TOKENS
    n_tiles = n // tm

    def tile(i):
        return jnp.clip(i - FFN_CHUNKS, 0, n_tiles - 1)

    def chunk(i):
        return jnp.minimum(i, FFN_CHUNKS - 1)

    return pl.pallas_call(
        functools.partial(_ffn_kernel, n_tiles),
        grid=(FFN_CHUNKS + n_tiles + 1,),
        in_specs=[
            pl.BlockSpec((tm, D_MODEL), lambda i: (tile(i), 0)),
            pl.BlockSpec((None, 3, None, 1, D_MODEL),
                         lambda i: (l, sub, tile(i) // tiles_per_seq, 0, 0)),
            _const_spec((n_sample, D_MODEL)),
            pl.BlockSpec((None, 3, n_sample, D_MODEL), lambda i: (l, sub, 0, 0),
                         pipeline_mode=pl.Buffered(1)),
            pl.BlockSpec((None, D_MODEL, FFN_CHUNK), lambda i: (l, 0, chunk(i))),
            pl.BlockSpec((None, D_MODEL, FFN_CHUNK), lambda i: (l, 0, chunk(i))),
            pl.BlockSpec((None, FFN_CHUNK, D_MODEL), lambda i: (l, chunk(i), 0)),
            pl.BlockSpec((None, None, 1, D_MODEL), lambda i: (l, sub, 0, 0)),
            pl.BlockSpec((None, None, 1, D_MODEL), lambda i: (l, sub, 0, 0)),
        ],
        out_specs=[
            pl.BlockSpec((tm, D_MODEL), lambda i: (tile(i), 0)),
            _whole_out_spec((n_sample, D_MODEL)),
        ],
        out_shape=[
            jax.ShapeDtypeStruct((n, D_MODEL), F32),
            jax.ShapeDtypeStruct((n_sample, D_MODEL), F32),
        ],
        scratch_shapes=[
            pltpu.VMEM((D_MODEL, D_FF), BF16),
            pltpu.VMEM((D_MODEL, D_FF), BF16),
            pltpu.VMEM((D_FF, D_MODEL), BF16),
        ],
        compiler_params=pltpu.CompilerParams(
            dimension_semantics=("arbitrary",), vmem_limit_bytes=VMEM_LIMIT_BYTES),
        name="ffn",
    )(xp, mod_p, xs, mod_s, wg, wu, wd, ln_g, ln_b)


def _prompt_mixer_kernel(tiles_per_seq, layer,
                         sink_ref, x_ref, m_ref, win_ref, convw_ref, poolw_ref, pscale_ref,
                         mixg_ref, wout_ref, lng_ref, lnb_ref,
                         o_ref, oconv_ref, opool_ref, ok_ref, ov_ref,
                         win_bf, wout_bf, cv_ext, u_ext, kd0, kd1, vd0, vd1, yattn):
    tm = x_ref.shape[0]
    tile = pl.program_id(0) % tiles_per_seq

    @pl.when(pl.program_id(0) == 0)
    def _():
        win_bf[...] = win_ref[...].astype(BF16)
        wout_bf[...] = wout_ref[...].astype(BF16)

    @pl.when(tile == 0)
    def _():
        cv_ext[0:CONV_PAD, :] = jnp.zeros((CONV_PAD, CONV_DIM), F32)
        u_ext[0:POOL_PAD, :] = jnp.zeros((POOL_PAD, POOL_DIM), F32)
        for ref in (kd0, kd1, vd0, vd1):
            ref[0:WINDOW, :] = jnp.zeros((WINDOW, KV_DIM), BF16)

    x = x_ref[...]
    shift, scale, gate = m_ref[0], m_ref[1], m_ref[2]
    h = (x * (1.0 + scale) + shift).astype(BF16)
    proj = jnp.dot(h, win_bf[...], preferred_element_type=F32)

    cv = proj[:, OFF_GC:OFF_GC + CONV_DIM] * proj[:, OFF_XIN:OFF_XIN + CONV_DIM]
    cv_ext[CONV_PAD:CONV_PAD + tm, :] = cv
    conv = convw_ref[CONV_WIDTH - 1:CONV_WIDTH, :] * cv
    for kk in range(CONV_WIDTH - 1):
        back = CONV_WIDTH - 1 - kk
        conv = conv + convw_ref[kk:kk + 1, :] * cv_ext[CONV_PAD - back:CONV_PAD - back + tm, :]
    y_conv = proj[:, OFF_GB:OFF_GB + CONV_DIM] * conv
    oconv_ref[...] = cv_ext[CONV_PAD + tm - (CONV_WIDTH - 1):CONV_PAD + tm, :]
    cv_ext[0:CONV_PAD, :] = cv_ext[tm:tm + CONV_PAD, :]

    u = proj[:, OFF_U:OFF_U + POOL_DIM]
    u_ext[POOL_PAD:POOL_PAD + tm, :] = u
    acc = u
    sums = {}
    for back in range(1, MAX_POOL):
        acc = acc + u_ext[POOL_PAD - back:POOL_PAD - back + tm, :]
        if back + 1 in POOL_WINDOWS:
            sums[back + 1] = acc
    pos1 = tile * tm + lax.broadcasted_iota(jnp.int32, (tm, 1), 0) + 1
    counts = [jnp.minimum(pos1, w).astype(F32) for w in POOL_WINDOWS]
    win_sum = _pool_select(*[sums[w] for w in POOL_WINDOWS])
    count = _pool_select(*[jnp.broadcast_to(c, (tm, POOL_DIM)) for c in counts])
    pooled = (win_sum / count - u).astype(BF16)
    y_pool = jnp.dot(pooled, poolw_ref[...], preferred_element_type=F32) * pscale_ref[...]
    opool_ref[...] = u_ext[POOL_PAD + tm - (MAX_POOL - 1):POOL_PAD + tm, :]
    u_ext[0:POOL_PAD, :] = u_ext[tm:tm + POOL_PAD, :]

    k = proj[:, OFF_K:OFF_K + KV_DIM]
    v = proj[:, OFF_V:OFF_V + KV_DIM]
    ok_ref[...] = k[tm - WINDOW:, :]
    ov_ref[...] = v[tm - WINDOW:, :]
    low = _low_half((tm, KV_DIM))
    k_sw = pltpu.roll(k, HALF_LANES, 1)
    v_sw = pltpu.roll(v, HALF_LANES, 1)
    kd0[WINDOW:WINDOW + tm, :] = jnp.where(low, k, k_sw).astype(BF16)
    kd1[WINDOW:WINDOW + tm, :] = jnp.where(low, k_sw, k).astype(BF16)
    vd0[WINDOW:WINDOW + tm, :] = jnp.where(low, v, v_sw).astype(BF16)
    vd1[WINDOW:WINDOW + tm, :] = jnp.where(low, v_sw, v).astype(BF16)

    rows = GQA_GROUP * WINDOW
    r = lax.broadcasted_iota(jnp.int32, (rows, 2 * WINDOW), 0)
    qidx = r % WINDOW
    kidx = lax.broadcasted_iota(jnp.int32, (rows, 2 * WINDOW), 1)
    band = (kidx > qidx) & (kidx <= qidx + WINDOW)
    first_key = jnp.where(tile == 0, WINDOW, 0)
    head_row = lax.broadcasted_iota(jnp.int32, (rows, 1), 0) // WINDOW
    low_q = _low_half((WINDOW, LANES))
    for g, (kd, vd) in enumerate(((kd0, vd0), (kd1, vd1))):
        sink = jnp.zeros((rows, 1), F32)
        for hh in range(GQA_GROUP):
            sink = jnp.where(head_row == hh, sink_ref[layer, g * GQA_GROUP + hh], sink)
        for j in range(tm // WINDOW):
            mask = (band & (kidx >= first_key)) if j == 0 else band
            q0 = OFF_Q + g * GQA_GROUP * HEAD_DIM
            qa = proj[j * WINDOW:(j + 1) * WINDOW, q0:q0 + LANES]
            qb = proj[j * WINDOW:(j + 1) * WINDOW, q0 + LANES:q0 + 2 * LANES]
            q4 = jnp.concatenate([jnp.where(low_q, qa, 0.0), jnp.where(low_q, 0.0, qa),
                                  jnp.where(low_q, qb, 0.0), jnp.where(low_q, 0.0, qb)],
                                 axis=0).astype(BF16)
            kk = kd[j * WINDOW:(j + 2) * WINDOW, :]
            vv = vd[j * WINDOW:(j + 2) * WINDOW, :]
            s = lax.dot_general(q4, kk, (((1,), (1,)), ((), ())), preferred_element_type=F32)
            s = jnp.where(mask, s * ATTN_SCALE, NEG_INF)
            mx = jnp.maximum(jnp.max(s, axis=-1, keepdims=True), sink)
            p = jnp.exp(s - mx)
            den = jnp.sum(p, axis=-1, keepdims=True) + jnp.exp(sink - mx)
            o = jnp.dot(p.astype(BF16), vv, preferred_element_type=F32) / den
            ya = jnp.where(low_q, o[0:WINDOW], o[WINDOW:2 * WINDOW])
            yb = jnp.where(low_q, o[2 * WINDOW:3 * WINDOW], o[3 * WINDOW:4 * WINDOW])
            c0 = g * GQA_GROUP * HEAD_DIM
            yattn[j * WINDOW:(j + 1) * WINDOW, c0:c0 + LANES] = ya
            yattn[j * WINDOW:(j + 1) * WINDOW, c0 + LANES:c0 + 2 * LANES] = yb
    for ref in (kd0, kd1, vd0, vd1):
        ref[0:WINDOW, :] = ref[tm:tm + WINDOW, :]

    mixg = mixg_ref[...]
    y = jnp.concatenate([
        _rms_norm(y_conv, mixg[:, 0:CONV_DIM]),
        _rms_norm(y_pool, mixg[:, CONV_DIM:CONV_DIM + POOL_DIM]),
        _rms_norm(yattn[...], mixg[:, CONV_DIM + POOL_DIM:]),
    ], axis=-1).astype(BF16)
    sub = jnp.dot(y, wout_bf[...], preferred_element_type=F32)
    o_ref[...] = _layer_norm(ALPHA * x + gate * sub, lng_ref[...], lnb_ref[...])


def _prompt_mixer(x, mod, l, sinks, w_in, conv_w, pool_wbd, pool_scale, mix_g, w_out,
                  ln_g, ln_b, batch, seq):
    tm = MIX_TOKENS
    tiles_per_seq = seq // tm
    n = x.shape[0]
    return pl.pallas_call(
        functools.partial(_prompt_mixer_kernel, tiles_per_seq, l),
        grid=(n // tm,),
        in_specs=[
            pl.BlockSpec(memory_space=pltpu.SMEM),
            pl.BlockSpec((tm, D_MODEL), lambda i: (i, 0)),
            pl.BlockSpec((None, 3, None, 1, D_MODEL), lambda i: (l, 1, i // tiles_per_seq, 0, 0)),
            _layer_spec((D_MODEL, D_IN), l),
            _layer_spec((CONV_WIDTH, CONV_DIM), l),
            _layer_spec((POOL_DIM, POOL_DIM), l),
            _layer_spec((1, POOL_DIM), l),
            _layer_spec((1, D_MODEL), l),
            _layer_spec((D_MODEL, D_MODEL), l),
            pl.BlockSpec((None, None, 1, D_MODEL), lambda i: (l, 1, 0, 0)),
            pl.BlockSpec((None, None, 1, D_MODEL), lambda i: (l, 1, 0, 0)),
        ],
        out_specs=[
            pl.BlockSpec((tm, D_MODEL), lambda i: (i, 0)),
            pl.BlockSpec((None, CONV_WIDTH - 1, CONV_DIM), lambda i: (i // tiles_per_seq, 0, 0)),
            pl.BlockSpec((None, MAX_POOL - 1, POOL_DIM), lambda i: (i // tiles_per_seq, 0, 0)),
            pl.BlockSpec((None, WINDOW, KV_DIM), lambda i: (i // tiles_per_seq, 0, 0)),
            pl.BlockSpec((None, WINDOW, KV_DIM), lambda i: (i // tiles_per_seq, 0, 0)),
        ],
        out_shape=[
            jax.ShapeDtypeStruct((n, D_MODEL), F32),
            jax.ShapeDtypeStruct((batch, CONV_WIDTH - 1, CONV_DIM), F32),
            jax.ShapeDtypeStruct((batch, MAX_POOL - 1, POOL_DIM), F32),
            jax.ShapeDtypeStruct((batch, WINDOW, KV_DIM), F32),
            jax.ShapeDtypeStruct((batch, WINDOW, KV_DIM), F32),
        ],
        scratch_shapes=[
            pltpu.VMEM((D_MODEL, D_IN), BF16),
            pltpu.VMEM((D_MODEL, D_MODEL), BF16),
            pltpu.VMEM((CONV_PAD + tm, CONV_DIM), F32),
            pltpu.VMEM((POOL_PAD + tm, POOL_DIM), F32),
            pltpu.VMEM((WINDOW + tm, KV_DIM), BF16),
            pltpu.VMEM((WINDOW + tm, KV_DIM), BF16),
            pltpu.VMEM((WINDOW + tm, KV_DIM), BF16),
            pltpu.VMEM((WINDOW + tm, KV_DIM), BF16),
            pltpu.VMEM((tm, ATTN_DIM), F32),
        ],
        compiler_params=pltpu.CompilerParams(
            dimension_semantics=("arbitrary",), vmem_limit_bytes=VMEM_LIMIT_BYTES),
        name="prompt_mixer",
    )(sinks, x, mod, w_in, conv_w, pool_wbd, pool_scale, mix_g, w_out, ln_g, ln_b)


def _sample_proj_kernel(x_ref, m_ref, win_ref, convw_ref, poolw_ref, pscale_ref, mixg_ref,
                        sconv_ref, spool_ref,
                        qm_ref, knew_ref, vnew_ref, ycp_ref, nconv_ref, npool_ref):
    x = x_ref[...]
    shift, scale = m_ref[0], m_ref[1]
    h = (x * (1.0 + scale) + shift).astype(BF16)
    proj = jnp.dot(h, win_ref[...].astype(BF16), preferred_element_type=F32)

    cv = proj[:, OFF_GC:OFF_GC + CONV_DIM] * proj[:, OFF_XIN:OFF_XIN + CONV_DIM]
    conv = convw_ref[CONV_WIDTH - 1:CONV_WIDTH, :] * cv
    for kk in range(CONV_WIDTH - 1):
        conv = conv + convw_ref[kk:kk + 1, :] * sconv_ref[:, kk * CONV_DIM:(kk + 1) * CONV_DIM]
    y_conv = proj[:, OFF_GB:OFF_GB + CONV_DIM] * conv
    keep = (CONV_WIDTH - 2) * CONV_DIM
    nconv_ref[:, 0:keep] = sconv_ref[:, CONV_DIM:CONV_DIM + keep]
    nconv_ref[:, keep:keep + CONV_DIM] = cv

    u = proj[:, OFF_U:OFF_U + POOL_DIM]
    acc = u
    sums = {}
    for back in range(1, MAX_POOL):
        acc = acc + spool_ref[MAX_POOL - 1 - back]
        if back + 1 in POOL_WINDOWS:
            sums[back + 1] = acc
    win_sum = _pool_select(*[sums[w] for w in POOL_WINDOWS])
    count = _pool_select(*[jnp.full(u.shape, float(w), F32) for w in POOL_WINDOWS])
    pooled = (win_sum / count - u).astype(BF16)
    y_pool = jnp.dot(pooled, poolw_ref[...], preferred_element_type=F32) * pscale_ref[...]
    npool_ref[0:MAX_POOL - 2] = spool_ref[1:MAX_POOL - 1]
    npool_ref[MAX_POOL - 2] = u

    mixg = mixg_ref[...]
    ycp_ref[:, 0:CONV_DIM] = _rms_norm(y_conv, mixg[:, 0:CONV_DIM])
    ycp_ref[:, CONV_DIM:CONV_DIM + POOL_DIM] = _rms_norm(y_pool, mixg[:, CONV_DIM:CONV_DIM + POOL_DIM])

    knew_ref[...] = proj[:, OFF_K:OFF_K + KV_DIM]
    vnew_ref[...] = proj[:, OFF_V:OFF_V + KV_DIM]
    low = _low_half((x.shape[0], LANES))
    for pair in range(N_HEADS // 2):
        g = (2 * pair) // GQA_GROUP
        qp = proj[:, OFF_Q + pair * LANES:OFF_Q + (pair + 1) * LANES]
        qs = pltpu.roll(qp, HALF_LANES, 1)
        if g == 0:
            first, second = jnp.where(low, qp, 0.0), jnp.where(low, qs, 0.0)
        else:
            first, second = jnp.where(low, 0.0, qs), jnp.where(low, 0.0, qp)
        qm_ref[:, (2 * pair) * LANES:(2 * pair + 1) * LANES] = first
        qm_ref[:, (2 * pair + 1) * LANES:(2 * pair + 2) * LANES] = second


def _sample_proj(x, mod_s, l, w_in, conv_w, pool_wbd, pool_scale, mix_g, sconv, spool):
    n = x.shape[0]
    return pl.pallas_call(
        _sample_proj_kernel,
        grid=(1,),
        in_specs=[
            _const_spec((n, D_MODEL)),
            pl.BlockSpec((None, 3, n, D_MODEL), lambda i: (l, 1, 0, 0)),
            _layer_spec((D_MODEL, D_IN), l),
            _layer_spec((CONV_WIDTH, CONV_DIM), l),
            _layer_spec((POOL_DIM, POOL_DIM), l),
            _layer_spec((1, POOL_DIM), l),
            _layer_spec((1, D_MODEL), l),
            _layer_spec((n, (CONV_WIDTH - 1) * CONV_DIM), l),
            _layer_spec((MAX_POOL - 1, n, POOL_DIM), l),
        ],
        out_specs=[
            _whole_out_spec((n, N_HEADS * LANES)),
            _whole_out_spec((n, KV_DIM)),
            _whole_out_spec((n, KV_DIM)),
            _whole_out_spec((n, CONV_DIM + POOL_DIM)),
            _whole_out_spec((n, (CONV_WIDTH - 1) * CONV_DIM)),
            _whole_out_spec((MAX_POOL - 1, n, POOL_DIM)),
        ],
        out_shape=[
            jax.ShapeDtypeStruct((n, N_HEADS * LANES), F32),
            jax.ShapeDtypeStruct((n, KV_DIM), F32),
            jax.ShapeDtypeStruct((n, KV_DIM), F32),
            jax.ShapeDtypeStruct((n, CONV_DIM + POOL_DIM), F32),
            jax.ShapeDtypeStruct((n, (CONV_WIDTH - 1) * CONV_DIM), F32),
            jax.ShapeDtypeStruct((MAX_POOL - 1, n, POOL_DIM), F32),
        ],
        compiler_params=pltpu.CompilerParams(
            dimension_semantics=("arbitrary",), vmem_limit_bytes=VMEM_LIMIT_BYTES),
        name="sample_proj",
    )(x, mod_s, w_in, conv_w, pool_wbd, pool_scale, mix_g, sconv, spool)


def _sample_attn_kernel(qm_ref, knew_ref, vnew_ref, ck_ref, cv_ref, sink_ref,
                        nk_ref, nv_ref, o_ref):
    nk_ref[:, 0:WINDOW - 1, :] = ck_ref[:, 1:WINDOW, :]
    nk_ref[:, WINDOW - 1:WINDOW, :] = knew_ref[...]
    nv_ref[:, 0:WINDOW - 1, :] = cv_ref[:, 1:WINDOW, :]
    nv_ref[:, WINDOW - 1:WINDOW, :] = vnew_ref[...]
    q = qm_ref[...].astype(BF16)
    s = jnp.einsum('bhd,bkd->bhk', q, nk_ref[...].astype(BF16),
                   preferred_element_type=F32) * ATTN_SCALE
    sink = sink_ref[...][None]
    mx = jnp.maximum(jnp.max(s, axis=-1, keepdims=True), sink)
    p = jnp.exp(s - mx)
    den = jnp.sum(p, axis=-1, keepdims=True) + jnp.exp(sink - mx)
    o = jnp.einsum('bhk,bkd->bhd', p.astype(BF16), nv_ref[...].astype(BF16),
                   preferred_element_type=F32)
    o_ref[...] = o / den


def _sample_attn(qm, knew, vnew, cache_k, cache_v, l, sinks):
    n = qm.shape[0]
    bb = SAMPLE_ATTN_BATCH
    row = pl.BlockSpec((bb, 1, KV_DIM), lambda i: (i, 0, 0))
    cache = pl.BlockSpec((None, bb, WINDOW, KV_DIM), lambda i: (l, i, 0, 0))
    win = pl.BlockSpec((bb, WINDOW, KV_DIM), lambda i: (i, 0, 0))
    heads = pl.BlockSpec((bb, N_HEADS, LANES), lambda i: (i, 0, 0))
    return pl.pallas_call(
        _sample_attn_kernel,
        grid=(n // bb,),
        in_specs=[heads, row, row, cache, cache, _layer_spec((N_HEADS, 1), l)],
        out_specs=[win, win, heads],
        out_shape=[
            jax.ShapeDtypeStruct((n, WINDOW, KV_DIM), F32),
            jax.ShapeDtypeStruct((n, WINDOW, KV_DIM), F32),
            jax.ShapeDtypeStruct((n, N_HEADS, LANES), F32),
        ],
        compiler_params=pltpu.CompilerParams(
            dimension_semantics=("arbitrary",), vmem_limit_bytes=VMEM_LIMIT_BYTES),
        name="sample_attn",
    )(qm, knew, vnew, cache_k, cache_v, sinks)


def _sample_out_kernel(x_ref, m_ref, ycp_ref, oh_ref, mixg_ref, wout_ref, lng_ref, lnb_ref, o_ref):
    x = x_ref[...]
    gate = m_ref[2]
    low = _low_half((x.shape[0], LANES))
    pairs = []
    for pair in range(N_HEADS // 2):
        g = (2 * pair) // GQA_GROUP
        oa = oh_ref[:, (2 * pair) * LANES:(2 * pair + 1) * LANES]
        ob = oh_ref[:, (2 * pair + 1) * LANES:(2 * pair + 2) * LANES]
        if g == 0:
            pairs.append(jnp.where(low, oa, pltpu.roll(ob, HALF_LANES, 1)))
        else:
            pairs.append(jnp.where(low, pltpu.roll(oa, HALF_LANES, 1), ob))
    y_attn = jnp.concatenate(pairs, axis=-1)
    mixg = mixg_ref[...]
    y = jnp.concatenate([ycp_ref[...], _rms_norm(y_attn, mixg[:, CONV_DIM + POOL_DIM:])],
                        axis=-1).astype(BF16)
    sub = jnp.dot(y, wout_ref[...].astype(BF16), preferred_element_type=F32)
    o_ref[...] = _layer_norm(ALPHA * x + gate * sub, lng_ref[...], lnb_ref[...])


def _sample_out(x, mod_s, l, ycp, oh, mix_g, w_out, ln_g, ln_b):
    n = x.shape[0]
    return pl.pallas_call(
        _sample_out_kernel,
        grid=(1,),
        in_specs=[
            _const_spec((n, D_MODEL)),
            pl.BlockSpec((None, 3, n, D_MODEL), lambda i: (l, 1, 0, 0)),
            _const_spec((n, CONV_DIM + POOL_DIM)),
            _const_spec((n, N_HEADS * LANES)),
            _layer_spec((1, D_MODEL), l),
            _layer_spec((D_MODEL, D_MODEL), l),
            pl.BlockSpec((None, None, 1, D_MODEL), lambda i: (l, 1, 0, 0)),
            pl.BlockSpec((None, None, 1, D_MODEL), lambda i: (l, 1, 0, 0)),
        ],
        out_specs=_whole_out_spec((n, D_MODEL)),
        out_shape=jax.ShapeDtypeStruct((n, D_MODEL), F32),
        compiler_params=pltpu.CompilerParams(
            dimension_semantics=("arbitrary",), vmem_limit_bytes=VMEM_LIMIT_BYTES),
        name="sample_out",
    )(x, mod_s, ycp, oh, mix_g, w_out, ln_g, ln_b)


def _block_diag(w):
    n, d, _ = w.shape
    eye = jnp.eye(n, dtype=w.dtype)
    return (eye[:, None, :, None] * w[:, :, None, :]).reshape(n * d, n * d)


def kernel(x_prompt, x_sample, state_conv, state_pool, cache_k_win, cache_v_win, c_prompt, c_sample, ln_g, ln_b, w_ada, b_ada, ffn1_gate, ffn1_up, ffn1_down, w_in, conv_w, pool_w, pool_scale, attn_sinks, mix_norm_g, w_out, ffn2_gate, ffn2_up, ffn2_down):
    batch, seq, _ = x_prompt.shape
    n_sample = x_sample.shape[0]
    win_buf = cache_k_win.shape[2]
    assert x_sample.shape[1] == 1 and win_buf == WINDOW
    assert seq % MIX_TOKENS == 0 and (batch * seq) % FFN_TOKENS == 0 and seq % FFN_TOKENS == 0
    assert n_sample % SAMPLE_ATTN_BATCH == 0 and batch <= ADA_ROWS_PAD

    c_all = jnp.concatenate(
        [c_prompt, jnp.zeros((ADA_ROWS_PAD - batch, D_MODEL), F32), c_sample], axis=0)
    mod_p, mod_s = _ada(c_all, w_ada, b_ada, batch, n_sample)
    mod_p = mod_p.reshape(DEPTH, N_MOD, batch, 1, D_MODEL)

    xp = x_prompt.reshape(batch * seq, D_MODEL)
    xs = x_sample.reshape(n_sample, D_MODEL)
    sconv = state_conv.reshape(DEPTH, n_sample, (CONV_WIDTH - 1) * CONV_DIM)
    spool = jnp.transpose(state_pool, (0, 2, 1, 3))
    cache_k = cache_k_win.reshape(DEPTH, n_sample, WINDOW, KV_DIM)
    cache_v = cache_v_win.reshape(DEPTH, n_sample, WINDOW, KV_DIM)
    pool_wbd = jax.vmap(_block_diag)(pool_w).astype(BF16)
    pscale = pool_scale.reshape(DEPTH, 1, POOL_DIM)
    mixg = mix_norm_g.reshape(DEPTH, 1, D_MODEL)
    lng = ln_g.reshape(DEPTH, 3, 1, D_MODEL)
    lnb = ln_b.reshape(DEPTH, 3, 1, D_MODEL)
    sinks_col = attn_sinks.reshape(DEPTH, N_HEADS, 1)

    ffn_tiles_per_seq = seq // FFN_TOKENS
    outs = {k: [] for k in ("pc", "pp", "pk", "pv", "sc", "sp", "sk", "sv")}
    for l in range(DEPTH):
        xp, xs = _ffn(xp, xs, mod_p, mod_s, l, 0, ffn_tiles_per_seq,
                      ffn1_gate, ffn1_up, ffn1_down, lng, lnb)

        xp, pc, pp, pk, pv = _prompt_mixer(
            xp, mod_p, l, attn_sinks, w_in, conv_w, pool_wbd, pscale, mixg, w_out,
            lng, lnb, batch, seq)

        qm, knew, vnew, ycp, nconv, npool = _sample_proj(
            xs, mod_s, l, w_in, conv_w, pool_wbd, pscale, mixg, sconv, spool)
        nk, nv, oh = _sample_attn(
            qm.reshape(n_sample, N_HEADS, LANES), knew.reshape(n_sample, 1, KV_DIM),
            vnew.reshape(n_sample, 1, KV_DIM), cache_k, cache_v, l, sinks_col)
        xs = _sample_out(xs, mod_s, l, ycp, oh.reshape(n_sample, N_HEADS * LANES), mixg, w_out,
                         lng, lnb)

        xp, xs = _ffn(xp, xs, mod_p, mod_s, l, 2, ffn_tiles_per_seq,
                      ffn2_gate, ffn2_up, ffn2_down, lng, lnb)

        for key, val in zip(("pc", "pp", "pk", "pv", "sc", "sp", "sk", "sv"),
                            (pc, pp, pk, pv, nconv, npool, nk, nv)):
            outs[key].append(val)

    kv_shape = (DEPTH, -1, WINDOW, N_KV_HEADS, HEAD_DIM)
    return (
        xp.reshape(batch, seq, D_MODEL),
        xs.reshape(n_sample, 1, D_MODEL),
        jnp.stack(outs["pc"]),
        jnp.stack(outs["pp"]),
        jnp.stack(outs["pk"]).reshape(kv_shape),
        jnp.stack(outs["pv"]).reshape(kv_shape),
        jnp.stack(outs["sc"]).reshape(DEPTH, n_sample, CONV_WIDTH - 1, CONV_DIM),
        jnp.transpose(jnp.stack(outs["sp"]), (0, 2, 1, 3)),
        jnp.stack(outs["sk"]).reshape(kv_shape),
        jnp.stack(outs["sv"]).reshape(kv_shape),
    )
```

```python
import functools

import jax
import jax.numpy as jnp
from jax import lax
from jax.experimental import pallas as pl
from jax.experimental.pallas import tpu as pltpu

D_MODEL = 1024
DEPTH = 4
CONV_DIM = 256
POOL_DIM = 256
ATTN_DIM = 512
HEAD_DIM = 64
CONV_WIDTH = 3
POOL_WINDOWS = (2, 4, 8, 16)
POOL_GROUP_DIM = 64
MAX_POOL = 16
N_HEADS = 8
N_KV_HEADS = 2
GQA_GROUP = 4
WINDOW = 128
D_FF = 2816
KV_DIM = 128
D_IN = 1792
N_MOD = 9
ALPHA = (2.0 * DEPTH) ** 0.25
LN_EPS = 1e-5
RMS_EPS = 1e-6
NEG_INF = -1e30
ATTN_SCALE = HEAD_DIM ** -0.5

OFF_GB, OFF_GC, OFF_XIN, OFF_U, OFF_Q, OFF_K, OFF_V = 0, 256, 512, 768, 1024, 1536, 1664

LANES = 128
HALF_LANES = 64
SUBLANES = 8
VMEM_LIMIT_BYTES = 56 * 1024 * 1024

ADA_ROWS_PAD = 8
FFN_TOKENS = 512
FFN_CHUNK = 256
FFN_CHUNKS = D_FF // FFN_CHUNK
MIX_TOKENS = 512
CONV_PAD = 8
POOL_GUARD = SUBLANES
POOL_PAD = POOL_GUARD + MAX_POOL
SAMPLE_ATTN_BATCH = 16

F32 = jnp.float32
BF16 = jnp.bfloat16


def _const_spec(shape):
    zeros = (0,) * len(shape)
    return pl.BlockSpec(shape, lambda *_: zeros, pipeline_mode=pl.Buffered(1))


def _layer_spec(shape, l):
    zeros = (0,) * len(shape)
    return pl.BlockSpec((None,) + tuple(shape), lambda *_: (l,) + zeros,
                        pipeline_mode=pl.Buffered(1))


def _whole_out_spec(shape):
    zeros = (0,) * len(shape)
    return pl.BlockSpec(shape, lambda *_: zeros)


def _layer_norm(y, g, b):
    mu = jnp.mean(y, axis=-1, keepdims=True)
    d = y - mu
    var = jnp.mean(d * d, axis=-1, keepdims=True)
    return d * lax.rsqrt(var + LN_EPS) * g + b


def _rms_norm(y, g):
    return y * lax.rsqrt(jnp.mean(y * y, axis=-1, keepdims=True) + RMS_EPS) * g


def _low_half(shape):
    return (lax.broadcasted_iota(jnp.int32, shape, len(shape) - 1) % LANES) < HALF_LANES


def _pool_select(s2, s4, s8, s16):
    grp = lax.broadcasted_iota(jnp.int32, s16.shape, 1) // POOL_GROUP_DIM
    return jnp.where(grp == 0, s2, jnp.where(grp == 1, s4, jnp.where(grp == 2, s8, s16)))


def _ada_kernel(c_ref, w_ref, b_ref, op_ref, os_ref):
    c = c_ref[...]
    a = (c * jax.nn.sigmoid(c)).astype(BF16)
    r = jnp.dot(a, w_ref[...].astype(BF16), preferred_element_type=F32) + b_ref[...]
    n_prompt = op_ref.shape[0]
    op_ref[...] = r[:n_prompt]
    os_ref[...] = r[ADA_ROWS_PAD:]


def _ada(c_all, w_ada, b_ada, n_prompt, n_sample):
    rows = c_all.shape[0]
    return pl.pallas_call(
        _ada_kernel,
        grid=(DEPTH, N_MOD),
        in_specs=[
            pl.BlockSpec((rows, D_MODEL), lambda l, j: (0, 0)),
            pl.BlockSpec((None, D_MODEL, D_MODEL), lambda l, j: (l, 0, j)),
            pl.BlockSpec((None, 1, D_MODEL), lambda l, j: (l, 0, j)),
        ],
        out_specs=[
            pl.BlockSpec((None, None, n_prompt, D_MODEL), lambda l, j: (l, j, 0, 0)),
            pl.BlockSpec((None, None, n_sample, D_MODEL), lambda l, j: (l, j, 0, 0)),
        ],
        out_shape=[
            jax.ShapeDtypeStruct((DEPTH, N_MOD, n_prompt, D_MODEL), F32),
            jax.ShapeDtypeStruct((DEPTH, N_MOD, n_sample, D_MODEL), F32),
        ],
        compiler_params=pltpu.CompilerParams(
            dimension_semantics=("arbitrary", "arbitrary"), vmem_limit_bytes=VMEM_LIMIT_BYTES),
        name="ada",
    )(c_all, w_ada, b_ada.reshape(DEPTH, 1, N_MOD * D_MODEL))


def _ffn_kernel(n_tiles,
                xp_ref, mp_ref, xs_ref, ms_ref, wg_ref, wu_ref, wd_ref, lng_ref, lnb_ref,
                op_ref, os_ref, wg_bf, wu_bf, wd_bf):
    i = pl.program_id(0)

    for c in range(FFN_CHUNKS):
        @pl.when(i == c)
        def _(c=c):
            cols = slice(c * FFN_CHUNK, (c + 1) * FFN_CHUNK)
            wg_bf[:, cols] = wg_ref[...].astype(BF16)
            wu_bf[:, cols] = wu_ref[...].astype(BF16)
            wd_bf[cols, :] = wd_ref[...].astype(BF16)

    def ffn(x_ref, m_ref, o_ref):
        x = x_ref[...]
        shift, scale, gate = m_ref[0], m_ref[1], m_ref[2]
        h = (x * (1.0 + scale) + shift).astype(BF16)
        g = jnp.dot(h, wg_bf[...], preferred_element_type=F32)
        u = jnp.dot(h, wu_bf[...], preferred_element_type=F32)
        a = (g * jax.nn.sigmoid(g) * u).astype(BF16)
        sub = 0.5 * jnp.dot(a, wd_bf[...], preferred_element_type=F32)
        o_ref[...] = _layer_norm(ALPHA * x + gate * sub, lng_ref[...], lnb_ref[...])

    @pl.when((i >= FFN_CHUNKS) & (i < FFN_CHUNKS + n_tiles))
    def _():
        ffn(xp_ref, mp_ref, op_ref)

    @pl.when(i == FFN_CHUNKS + n_tiles)
    def _():
        ffn(xs_ref, ms_ref, os_ref)


def _ffn(xp, xs, mod_p, mod_s, l, sub, tiles_per_seq, wg, wu, wd, ln_g, ln_b):
    n, n_sample = xp.shape[0], xs.shape[0]
    tm = FFN_TOKENS
    n_tiles = n // tm

    def tile(i):
        return jnp.clip(i - FFN_CHUNKS, 0, n_tiles - 1)

    def chunk(i):
        return jnp.minimum(i, FFN_CHUNKS - 1)

    return pl.pallas_call(
        functools.partial(_ffn_kernel, n_tiles),
        grid=(FFN_CHUNKS + n_tiles + 1,),
        in_specs=[
            pl.BlockSpec((tm, D_MODEL), lambda i: (tile(i), 0)),
            pl.BlockSpec((None, 3, None, 1, D_MODEL),
                         lambda i: (l, sub, tile(i) // tiles_per_seq, 0, 0)),
            _const_spec((n_sample, D_MODEL)),
            pl.BlockSpec((None, 3, n_sample, D_MODEL), lambda i: (l, sub, 0, 0),
                         pipeline_mode=pl.Buffered(1)),
            pl.BlockSpec((None, D_MODEL, FFN_CHUNK), lambda i: (l, 0, chunk(i))),
            pl.BlockSpec((None, D_MODEL, FFN_CHUNK), lambda i: (l, 0, chunk(i))),
            pl.BlockSpec((None, FFN_CHUNK, D_MODEL), lambda i: (l, chunk(i), 0)),
            pl.BlockSpec((None, None, 1, D_MODEL), lambda i: (l, sub, 0, 0)),
            pl.BlockSpec((None, None, 1, D_MODEL), lambda i: (l, sub, 0, 0)),
        ],
        out_specs=[
            pl.BlockSpec((tm, D_MODEL), lambda i: (tile(i), 0)),
            _whole_out_spec((n_sample, D_MODEL)),
        ],
        out_shape=[
            jax.ShapeDtypeStruct((n, D_MODEL), F32),
            jax.ShapeDtypeStruct((n_sample, D_MODEL), F32),
        ],
        scratch_shapes=[
            pltpu.VMEM((D_MODEL, D_FF), BF16),
            pltpu.VMEM((D_MODEL, D_FF), BF16),
            pltpu.VMEM((D_FF, D_MODEL), BF16),
        ],
        compiler_params=pltpu.CompilerParams(
            dimension_semantics=("arbitrary",), vmem_limit_bytes=VMEM_LIMIT_BYTES),
        name="ffn",
    )(xp, mod_p, xs, mod_s, wg, wu, wd, ln_g, ln_b)


def _prompt_mixer_kernel(tiles_per_seq, layer,
                         sink_ref, x_ref, m_ref, win_ref, convw_ref, poolw_ref, pscale_ref,
                         mixg_ref, wout_ref, lng_ref, lnb_ref,
                         o_ref, oconv_ref, opool_ref, ok_ref, ov_ref,
                         win_bf, wout_bf, cv_ext, u_ext, s2_ext, s4_ext, s8_ext,
                         kd0, kd1, vd0, vd1, yattn):
    tm = x_ref.shape[0]
    tile = pl.program_id(0) % tiles_per_seq

    @pl.when(pl.program_id(0) == 0)
    def _():
        win_bf[...] = win_ref[...].astype(BF16)
        wout_bf[...] = wout_ref[...].astype(BF16)

    @pl.when(tile == 0)
    def _():
        cv_ext[0:CONV_PAD, :] = jnp.zeros((CONV_PAD, CONV_DIM), F32)
        u_ext[0:POOL_PAD, :] = jnp.zeros((POOL_PAD, POOL_DIM), F32)
        s2_ext[0:POOL_GUARD, :] = jnp.zeros((POOL_GUARD, POOL_DIM), F32)
        s4_ext[0:POOL_GUARD, :] = jnp.zeros((POOL_GUARD, POOL_DIM), F32)
        for ref in (kd0, kd1, vd0, vd1):
            ref[0:WINDOW, :] = jnp.zeros((WINDOW, KV_DIM), BF16)

    x = x_ref[...]
    shift, scale, gate = m_ref[0], m_ref[1], m_ref[2]
    h = (x * (1.0 + scale) + shift).astype(BF16)
    proj = jnp.dot(h, win_bf[...], preferred_element_type=F32)

    cv = proj[:, OFF_GC:OFF_GC + CONV_DIM] * proj[:, OFF_XIN:OFF_XIN + CONV_DIM]
    cv_ext[CONV_PAD:CONV_PAD + tm, :] = cv
    conv = convw_ref[CONV_WIDTH - 1:CONV_WIDTH, :] * cv
    for kk in range(CONV_WIDTH - 1):
        back = CONV_WIDTH - 1 - kk
        conv = conv + convw_ref[kk:kk + 1, :] * cv_ext[CONV_PAD - back:CONV_PAD - back + tm, :]
    y_conv = proj[:, OFF_GB:OFF_GB + CONV_DIM] * conv
    oconv_ref[...] = cv_ext[CONV_PAD + tm - (CONV_WIDTH - 1):CONV_PAD + tm, :]
    cv_ext[0:CONV_PAD, :] = cv_ext[tm:tm + CONV_PAD, :]

    u = proj[:, OFF_U:OFF_U + POOL_DIM]
    u_ext[POOL_PAD:POOL_PAD + tm, :] = u
    lo, n_ext = POOL_GUARD, POOL_PAD - POOL_GUARD + tm
    sums = {1: u_ext}
    for w, dst in ((1, s2_ext), (2, s4_ext), (4, s8_ext)):
        src = sums[w]
        dst[lo:lo + n_ext, :] = src[lo:lo + n_ext, :] + src[lo - w:lo - w + n_ext, :]
        sums[2 * w] = dst
    half = MAX_POOL // 2
    s16 = s8_ext[POOL_PAD:POOL_PAD + tm, :] + s8_ext[POOL_PAD - half:POOL_PAD - half + tm, :]
    sums = {w: sums[w][POOL_PAD:POOL_PAD + tm, :] for w in (2, 4, 8)}
    sums[16] = s16
    pos1 = tile * tm + lax.broadcasted_iota(jnp.int32, (tm, 1), 0) + 1
    counts = [jnp.minimum(pos1, w).astype(F32) for w in POOL_WINDOWS]
    win_sum = _pool_select(*[sums[w] for w in POOL_WINDOWS])
    count = _pool_select(*[jnp.broadcast_to(c, (tm, POOL_DIM)) for c in counts])
    pooled = (win_sum / count - u).astype(BF16)
    y_pool = jnp.dot(pooled, poolw_ref[...], preferred_element_type=F32) * pscale_ref[...]
    opool_ref[...] = u_ext[POOL_PAD + tm - (MAX_POOL - 1):POOL_PAD + tm, :]
    u_ext[POOL_GUARD:POOL_PAD, :] = u_ext[tm + POOL_GUARD:tm + POOL_PAD, :]

    k = proj[:, OFF_K:OFF_K + KV_DIM]
    v = proj[:, OFF_V:OFF_V + KV_DIM]
    ok_ref[...] = k[tm - WINDOW:, :]
    ov_ref[...] = v[tm - WINDOW:, :]
    low = _low_half((tm, KV_DIM))
    k_sw = pltpu.roll(k, HALF_LANES, 1)
    v_sw = pltpu.roll(v, HALF_LANES, 1)
    kd0[WINDOW:WINDOW + tm, :] = jnp.where(low, k, k_sw).astype(BF16)
    kd1[WINDOW:WINDOW + tm, :] = jnp.where(low, k_sw, k).astype(BF16)
    vd0[WINDOW:WINDOW + tm, :] = jnp.where(low, v, v_sw).astype(BF16)
    vd1[WINDOW:WINDOW + tm, :] = jnp.where(low, v_sw, v).astype(BF16)

    rows = GQA_GROUP * WINDOW
    r = lax.broadcasted_iota(jnp.int32, (rows, 2 * WINDOW), 0)
    qidx = r % WINDOW
    kidx = lax.broadcasted_iota(jnp.int32, (rows, 2 * WINDOW), 1)
    band = (kidx > qidx) & (kidx <= qidx + WINDOW)
    first_key = jnp.where(tile == 0, WINDOW, 0)
    head_row = lax.broadcasted_iota(jnp.int32, (rows, 1), 0) // WINDOW
    low_q = _low_half((WINDOW, LANES))
    for g, (kd, vd) in enumerate(((kd0, vd0), (kd1, vd1))):
        sink = jnp.zeros((rows, 1), F32)
        for hh in range(GQA_GROUP):
            sink = jnp.where(head_row == hh, sink_ref[layer, g * GQA_GROUP + hh], sink)
        for j in range(tm // WINDOW):
            mask = (band & (kidx >= first_key)) if j == 0 else band
            q0 = OFF_Q + g * GQA_GROUP * HEAD_DIM
            qa = proj[j * WINDOW:(j + 1) * WINDOW, q0:q0 + LANES] * ATTN_SCALE
            qb = proj[j * WINDOW:(j + 1) * WINDOW, q0 + LANES:q0 + 2 * LANES] * ATTN_SCALE
            q4 = jnp.concatenate([jnp.where(low_q, qa, 0.0), jnp.where(low_q, 0.0, qa),
                                  jnp.where(low_q, qb, 0.0), jnp.where(low_q, 0.0, qb)],
                                 axis=0).astype(BF16)
            kk = kd[j * WINDOW:(j + 2) * WINDOW, :]
            vv = vd[j * WINDOW:(j + 2) * WINDOW, :]
            s = lax.dot_general(q4, kk, (((1,), (1,)), ((), ())), preferred_element_type=F32)
            s = jnp.where(mask, s, NEG_INF)
            mx = jnp.maximum(jnp.max(s, axis=-1, keepdims=True), sink)
            p = jnp.exp(s - mx)
            den = jnp.sum(p, axis=-1, keepdims=True) + jnp.exp(sink - mx)
            o = jnp.dot(p.astype(BF16), vv, preferred_element_type=F32) / den
            ya = jnp.where(low_q, o[0:WINDOW], o[WINDOW:2 * WINDOW])
            yb = jnp.where(low_q, o[2 * WINDOW:3 * WINDOW], o[3 * WINDOW:4 * WINDOW])
            c0 = g * GQA_GROUP * HEAD_DIM
            yattn[j * WINDOW:(j + 1) * WINDOW, c0:c0 + LANES] = ya
            yattn[j * WINDOW:(j + 1) * WINDOW, c0 + LANES:c0 + 2 * LANES] = yb
    for ref in (kd0, kd1, vd0, vd1):
        ref[0:WINDOW, :] = ref[tm:tm + WINDOW, :]

    mixg = mixg_ref[...]
    y = jnp.concatenate([
        _rms_norm(y_conv, mixg[:, 0:CONV_DIM]),
        _rms_norm(y_pool, mixg[:, CONV_DIM:CONV_DIM + POOL_DIM]),
        _rms_norm(yattn[...], mixg[:, CONV_DIM + POOL_DIM:]),
    ], axis=-1).astype(BF16)
    sub = jnp.dot(y, wout_bf[...], preferred_element_type=F32)
    o_ref[...] = _layer_norm(ALPHA * x + gate * sub, lng_ref[...], lnb_ref[...])


def _prompt_mixer(x, mod, l, sinks, w_in, conv_w, pool_wbd, pool_scale, mix_g, w_out,
                  ln_g, ln_b, batch, seq):
    tm = MIX_TOKENS
    tiles_per_seq = seq // tm
    n = x.shape[0]
    return pl.pallas_call(
        functools.partial(_prompt_mixer_kernel, tiles_per_seq, l),
        grid=(n // tm,),
        in_specs=[
            pl.BlockSpec(memory_space=pltpu.SMEM),
            pl.BlockSpec((tm, D_MODEL), lambda i: (i, 0)),
            pl.BlockSpec((None, 3, None, 1, D_MODEL), lambda i: (l, 1, i // tiles_per_seq, 0, 0)),
            _layer_spec((D_MODEL, D_IN), l),
            _layer_spec((CONV_WIDTH, CONV_DIM), l),
            _layer_spec((POOL_DIM, POOL_DIM), l),
            _layer_spec((1, POOL_DIM), l),
            _layer_spec((1, D_MODEL), l),
            _layer_spec((D_MODEL, D_MODEL), l),
            pl.BlockSpec((None, None, 1, D_MODEL), lambda i: (l, 1, 0, 0)),
            pl.BlockSpec((None, None, 1, D_MODEL), lambda i: (l, 1, 0, 0)),
        ],
        out_specs=[
            pl.BlockSpec((tm, D_MODEL), lambda i: (i, 0)),
            pl.BlockSpec((None, CONV_WIDTH - 1, CONV_DIM), lambda i: (i // tiles_per_seq, 0, 0)),
            pl.BlockSpec((None, MAX_POOL - 1, POOL_DIM), lambda i: (i // tiles_per_seq, 0, 0)),
            pl.BlockSpec((None, WINDOW, KV_DIM), lambda i: (i // tiles_per_seq, 0, 0)),
            pl.BlockSpec((None, WINDOW, KV_DIM), lambda i: (i // tiles_per_seq, 0, 0)),
        ],
        out_shape=[
            jax.ShapeDtypeStruct((n, D_MODEL), F32),
            jax.ShapeDtypeStruct((batch, CONV_WIDTH - 1, CONV_DIM), F32),
            jax.ShapeDtypeStruct((batch, MAX_POOL - 1, POOL_DIM), F32),
            jax.ShapeDtypeStruct((batch, WINDOW, KV_DIM), F32),
            jax.ShapeDtypeStruct((batch, WINDOW, KV_DIM), F32),
        ],
        scratch_shapes=[
            pltpu.VMEM((D_MODEL, D_IN), BF16),
            pltpu.VMEM((D_MODEL, D_MODEL), BF16),
            pltpu.VMEM((CONV_PAD + tm, CONV_DIM), F32),
            pltpu.VMEM((POOL_PAD + tm, POOL_DIM), F32),
            pltpu.VMEM((POOL_PAD + tm, POOL_DIM), F32),
            pltpu.VMEM((POOL_PAD + tm, POOL_DIM), F32),
            pltpu.VMEM((POOL_PAD + tm, POOL_DIM), F32),
            pltpu.VMEM((WINDOW + tm, KV_DIM), BF16),
            pltpu.VMEM((WINDOW + tm, KV_DIM), BF16),
            pltpu.VMEM((WINDOW + tm, KV_DIM), BF16),
            pltpu.VMEM((WINDOW + tm, KV_DIM), BF16),
            pltpu.VMEM((tm, ATTN_DIM), F32),
        ],
        compiler_params=pltpu.CompilerParams(
            dimension_semantics=("arbitrary",), vmem_limit_bytes=VMEM_LIMIT_BYTES),
        name="prompt_mixer",
    )(sinks, x, mod, w_in, conv_w, pool_wbd, pool_scale, mix_g, w_out, ln_g, ln_b)


def _sample_proj_kernel(x_ref, m_ref, win_ref, convw_ref, poolw_ref, pscale_ref, mixg_ref,
                        sconv_ref, spool_ref,
                        qm_ref, knew_ref, vnew_ref, ycp_ref, nconv_ref, npool_ref):
    x = x_ref[...]
    shift, scale = m_ref[0], m_ref[1]
    h = (x * (1.0 + scale) + shift).astype(BF16)
    proj = jnp.dot(h, win_ref[...].astype(BF16), preferred_element_type=F32)

    cv = proj[:, OFF_GC:OFF_GC + CONV_DIM] * proj[:, OFF_XIN:OFF_XIN + CONV_DIM]
    conv = convw_ref[CONV_WIDTH - 1:CONV_WIDTH, :] * cv
    for kk in range(CONV_WIDTH - 1):
        conv = conv + convw_ref[kk:kk + 1, :] * sconv_ref[:, kk * CONV_DIM:(kk + 1) * CONV_DIM]
    y_conv = proj[:, OFF_GB:OFF_GB + CONV_DIM] * conv
    keep = (CONV_WIDTH - 2) * CONV_DIM
    nconv_ref[:, 0:keep] = sconv_ref[:, CONV_DIM:CONV_DIM + keep]
    nconv_ref[:, keep:keep + CONV_DIM] = cv

    u = proj[:, OFF_U:OFF_U + POOL_DIM]
    acc = u
    sums = {}
    for back in range(1, MAX_POOL):
        acc = acc + spool_ref[MAX_POOL - 1 - back]
        if back + 1 in POOL_WINDOWS:
            sums[back + 1] = acc
    win_sum = _pool_select(*[sums[w] for w in POOL_WINDOWS])
    count = _pool_select(*[jnp.full(u.shape, float(w), F32) for w in POOL_WINDOWS])
    pooled = (win_sum / count - u).astype(BF16)
    y_pool = jnp.dot(pooled, poolw_ref[...], preferred_element_type=F32) * pscale_ref[...]
    npool_ref[0:MAX_POOL - 2] = spool_ref[1:MAX_POOL - 1]
    npool_ref[MAX_POOL - 2] = u

    mixg = mixg_ref[...]
    ycp_ref[:, 0:CONV_DIM] = _rms_norm(y_conv, mixg[:, 0:CONV_DIM])
    ycp_ref[:, CONV_DIM:CONV_DIM + POOL_DIM] = _rms_norm(y_pool, mixg[:, CONV_DIM:CONV_DIM + POOL_DIM])

    knew_ref[...] = proj[:, OFF_K:OFF_K + KV_DIM].T
    vnew_ref[...] = proj[:, OFF_V:OFF_V + KV_DIM].T
    low = _low_half((x.shape[0], LANES))
    for pair in range(N_HEADS // 2):
        g = (2 * pair) // GQA_GROUP
        qp = proj[:, OFF_Q + pair * LANES:OFF_Q + (pair + 1) * LANES]
        qs = pltpu.roll(qp, HALF_LANES, 1)
        if g == 0:
            first, second = jnp.where(low, qp, 0.0), jnp.where(low, qs, 0.0)
        else:
            first, second = jnp.where(low, 0.0, qs), jnp.where(low, 0.0, qp)
        qm_ref[:, (2 * pair) * LANES:(2 * pair + 1) * LANES] = first
        qm_ref[:, (2 * pair + 1) * LANES:(2 * pair + 2) * LANES] = second


def _sample_proj(x, mod_s, l, w_in, conv_w, pool_wbd, pool_scale, mix_g, sconv, spool):
    n = x.shape[0]
    return pl.pallas_call(
        _sample_proj_kernel,
        grid=(1,),
        in_specs=[
            _const_spec((n, D_MODEL)),
            pl.BlockSpec((None, 3, n, D_MODEL), lambda i: (l, 1, 0, 0)),
            _layer_spec((D_MODEL, D_IN), l),
            _layer_spec((CONV_WIDTH, CONV_DIM), l),
            _layer_spec((POOL_DIM, POOL_DIM), l),
            _layer_spec((1, POOL_DIM), l),
            _layer_spec((1, D_MODEL), l),
            _layer_spec((n, (CONV_WIDTH - 1) * CONV_DIM), l),
            _layer_spec((MAX_POOL - 1, n, POOL_DIM), l),
        ],
        out_specs=[
            _whole_out_spec((n, N_HEADS * LANES)),
            _whole_out_spec((KV_DIM, n)),
            _whole_out_spec((KV_DIM, n)),
            _whole_out_spec((n, CONV_DIM + POOL_DIM)),
            _whole_out_spec((n, (CONV_WIDTH - 1) * CONV_DIM)),
            _whole_out_spec((MAX_POOL - 1, n, POOL_DIM)),
        ],
        out_shape=[
            jax.ShapeDtypeStruct((n, N_HEADS * LANES), F32),
            jax.ShapeDtypeStruct((KV_DIM, n), F32),
            jax.ShapeDtypeStruct((KV_DIM, n), F32),
            jax.ShapeDtypeStruct((n, CONV_DIM + POOL_DIM), F32),
            jax.ShapeDtypeStruct((n, (CONV_WIDTH - 1) * CONV_DIM), F32),
            jax.ShapeDtypeStruct((MAX_POOL - 1, n, POOL_DIM), F32),
        ],
        compiler_params=pltpu.CompilerParams(
            dimension_semantics=("arbitrary",), vmem_limit_bytes=VMEM_LIMIT_BYTES),
        name="sample_proj",
    )(x, mod_s, w_in, conv_w, pool_wbd, pool_scale, mix_g, sconv, spool)


def _sample_attn_kernel(qm_ref, knew_ref, vnew_ref, ck_ref, cv_ref, sink_ref,
                        nk_ref, nv_ref, o_ref):
    bb = qm_ref.shape[0]
    first_seq = pl.program_id(0) * bb
    newest = lax.broadcasted_iota(jnp.int32, (KV_DIM, WINDOW), 1) == WINDOW - 1
    seq_lane = lax.broadcasted_iota(jnp.int32, knew_ref.shape, 1)
    for b in range(bb):
        mine = seq_lane == first_seq + b
        for new_ref, old_ref, out_ref in ((knew_ref, ck_ref, nk_ref), (vnew_ref, cv_ref, nv_ref)):
            col = jnp.sum(jnp.where(mine, new_ref[...], 0.0), axis=1, keepdims=True)
            out_ref[b] = jnp.where(newest, col, pltpu.roll(old_ref[b], WINDOW - 1, 1))
    q = qm_ref[...].astype(BF16)
    s = jnp.einsum('bhc,bck->bhk', q, nk_ref[...].astype(BF16),
                   preferred_element_type=F32) * ATTN_SCALE
    sink = sink_ref[...][None]
    mx = jnp.maximum(jnp.max(s, axis=-1, keepdims=True), sink)
    p = jnp.exp(s - mx)
    den = jnp.sum(p, axis=-1, keepdims=True) + jnp.exp(sink - mx)
    o = jnp.einsum('bhk,bck->bhc', p.astype(BF16), nv_ref[...].astype(BF16),
                   preferred_element_type=F32)
    o_ref[...] = o / den


def _sample_attn(qm, knew_t, vnew_t, cache_k, cache_v, l, sinks):
    n = qm.shape[0]
    bb = SAMPLE_ATTN_BATCH
    cache = pl.BlockSpec((None, bb, KV_DIM, WINDOW), lambda i: (l, i, 0, 0))
    win = pl.BlockSpec((bb, KV_DIM, WINDOW), lambda i: (i, 0, 0))
    heads = pl.BlockSpec((bb, N_HEADS, LANES), lambda i: (i, 0, 0))
    return pl.pallas_call(
        _sample_attn_kernel,
        grid=(n // bb,),
        in_specs=[heads, _const_spec((KV_DIM, n)), _const_spec((KV_DIM, n)), cache, cache,
                  _layer_spec((N_HEADS, 1), l)],
        out_specs=[win, win, heads],
        out_shape=[
            jax.ShapeDtypeStruct((n, KV_DIM, WINDOW), F32),
            jax.ShapeDtypeStruct((n, KV_DIM, WINDOW), F32),
            jax.ShapeDtypeStruct((n, N_HEADS, LANES), F32),
        ],
        compiler_params=pltpu.CompilerParams(
            dimension_semantics=("arbitrary",), vmem_limit_bytes=VMEM_LIMIT_BYTES),
        name="sample_attn",
    )(qm, knew_t, vnew_t, cache_k, cache_v, sinks)


def _sample_out_kernel(x_ref, m_ref, ycp_ref, oh_ref, mixg_ref, wout_ref, lng_ref, lnb_ref, o_ref):
    x = x_ref[...]
    gate = m_ref[2]
    low = _low_half((x.shape[0], LANES))
    pairs = []
    for pair in range(N_HEADS // 2):
        g = (2 * pair) // GQA_GROUP
        oa = oh_ref[:, (2 * pair) * LANES:(2 * pair + 1) * LANES]
        ob = oh_ref[:, (2 * pair + 1) * LANES:(2 * pair + 2) * LANES]
        if g == 0:
            pairs.append(jnp.where(low, oa, pltpu.roll(ob, HALF_LANES, 1)))
        else:
            pairs.append(jnp.where(low, pltpu.roll(oa, HALF_LANES, 1), ob))
    y_attn = jnp.concatenate(pairs, axis=-1)
    mixg = mixg_ref[...]
    y = jnp.concatenate([ycp_ref[...], _rms_norm(y_attn, mixg[:, CONV_DIM + POOL_DIM:])],
                        axis=-1).astype(BF16)
    sub = jnp.dot(y, wout_ref[...].astype(BF16), preferred_element_type=F32)
    o_ref[...] = _layer_norm(ALPHA * x + gate * sub, lng_ref[...], lnb_ref[...])


def _sample_out(x, mod_s, l, ycp, oh, mix_g, w_out, ln_g, ln_b):
    n = x.shape[0]
    return pl.pallas_call(
        _sample_out_kernel,
        grid=(1,),
        in_specs=[
            _const_spec((n, D_MODEL)),
            pl.BlockSpec((None, 3, n, D_MODEL), lambda i: (l, 1, 0, 0)),
            _const_spec((n, CONV_DIM + POOL_DIM)),
            _const_spec((n, N_HEADS * LANES)),
            _layer_spec((1, D_MODEL), l),
            _layer_spec((D_MODEL, D_MODEL), l),
            pl.BlockSpec((None, None, 1, D_MODEL), lambda i: (l, 1, 0, 0)),
            pl.BlockSpec((None, None, 1, D_MODEL), lambda i: (l, 1, 0, 0)),
        ],
        out_specs=_whole_out_spec((n, D_MODEL)),
        out_shape=jax.ShapeDtypeStruct((n, D_MODEL), F32),
        compiler_params=pltpu.CompilerParams(
            dimension_semantics=("arbitrary",), vmem_limit_bytes=VMEM_LIMIT_BYTES),
        name="sample_out",
    )(x, mod_s, ycp, oh, mix_g, w_out, ln_g, ln_b)


def _block_diag(w):
    n, d, _ = w.shape
    eye = jnp.eye(n, dtype=w.dtype)
    return (eye[:, None, :, None] * w[:, :, None, :]).reshape(n * d, n * d)


def kernel(x_prompt, x_sample, state_conv, state_pool, cache_k_win, cache_v_win, c_prompt, c_sample, ln_g, ln_b, w_ada, b_ada, ffn1_gate, ffn1_up, ffn1_down, w_in, conv_w, pool_w, pool_scale, attn_sinks, mix_norm_g, w_out, ffn2_gate, ffn2_up, ffn2_down):
    batch, seq, _ = x_prompt.shape
    n_sample = x_sample.shape[0]
    win_buf = cache_k_win.shape[2]
    assert x_sample.shape[1] == 1 and win_buf == WINDOW
    assert seq % MIX_TOKENS == 0 and (batch * seq) % FFN_TOKENS == 0 and seq % FFN_TOKENS == 0
    assert n_sample % SAMPLE_ATTN_BATCH == 0 and batch <= ADA_ROWS_PAD

    c_all = jnp.concatenate(
        [c_prompt, jnp.zeros((ADA_ROWS_PAD - batch, D_MODEL), F32), c_sample], axis=0)
    mod_p, mod_s = _ada(c_all, w_ada, b_ada, batch, n_sample)
    mod_p = mod_p.reshape(DEPTH, N_MOD, batch, 1, D_MODEL)

    xp = x_prompt.reshape(batch * seq, D_MODEL)
    xs = x_sample.reshape(n_sample, D_MODEL)
    sconv = state_conv.reshape(DEPTH, n_sample, (CONV_WIDTH - 1) * CONV_DIM)
    spool = jnp.transpose(state_pool, (0, 2, 1, 3))
    cache_k = jnp.transpose(cache_k_win, (0, 1, 3, 4, 2)).reshape(DEPTH, n_sample, KV_DIM, WINDOW)
    cache_v = jnp.transpose(cache_v_win, (0, 1, 3, 4, 2)).reshape(DEPTH, n_sample, KV_DIM, WINDOW)
    pool_wbd = jax.vmap(_block_diag)(pool_w).astype(BF16)
    pscale = pool_scale.reshape(DEPTH, 1, POOL_DIM)
    mixg = mix_norm_g.reshape(DEPTH, 1, D_MODEL)
    lng = ln_g.reshape(DEPTH, 3, 1, D_MODEL)
    lnb = ln_b.reshape(DEPTH, 3, 1, D_MODEL)
    sinks_col = attn_sinks.reshape(DEPTH, N_HEADS, 1)

    ffn_tiles_per_seq = seq // FFN_TOKENS
    outs = {k: [] for k in ("pc", "pp", "pk", "pv", "sc", "sp", "sk", "sv")}
    for l in range(DEPTH):
        xp, xs = _ffn(xp, xs, mod_p, mod_s, l, 0, ffn_tiles_per_seq,
                      ffn1_gate, ffn1_up, ffn1_down, lng, lnb)

        xp, pc, pp, pk, pv = _prompt_mixer(
            xp, mod_p, l, attn_sinks, w_in, conv_w, pool_wbd, pscale, mixg, w_out,
            lng, lnb, batch, seq)

        qm, knew, vnew, ycp, nconv, npool = _sample_proj(
            xs, mod_s, l, w_in, conv_w, pool_wbd, pscale, mixg, sconv, spool)
        nk, nv, oh = _sample_attn(
            qm.reshape(n_sample, N_HEADS, LANES), knew, vnew, cache_k, cache_v, l, sinks_col)
        xs = _sample_out(xs, mod_s, l, ycp, oh.reshape(n_sample, N_HEADS * LANES), mixg, w_out,
                         lng, lnb)

        xp, xs = _ffn(xp, xs, mod_p, mod_s, l, 2, ffn_tiles_per_seq,
                      ffn2_gate, ffn2_up, ffn2_down, lng, lnb)

        for key, val in zip(("pc", "pp", "pk", "pv", "sc", "sp", "sk", "sv"),
                            (pc, pp, pk, pv, nconv, npool, nk, nv)):
            outs[key].append(val)

    kv_shape = (DEPTH, -1, WINDOW, N_KV_HEADS, HEAD_DIM)
    kv_t_shape = (DEPTH, -1, N_KV_HEADS, HEAD_DIM, WINDOW)
    return (
        xp.reshape(batch, seq, D_MODEL),
        xs.reshape(n_sample, 1, D_MODEL),
        jnp.stack(outs["pc"]),
        jnp.stack(outs["pp"]),
        jnp.stack(outs["pk"]).reshape(kv_shape),
        jnp.stack(outs["pv"]).reshape(kv_shape),
        jnp.stack(outs["sc"]).reshape(DEPTH, n_sample, CONV_WIDTH - 1, CONV_DIM),
        jnp.transpose(jnp.stack(outs["sp"]), (0, 2, 1, 3)),
        jnp.transpose(jnp.stack(outs["sk"]).reshape(kv_t_shape), (0, 1, 4, 2, 3)),
        jnp.transpose(jnp.stack(outs["sv"]).reshape(kv_t_shape), (0, 1, 4, 2, 3)),
    )
```

```python
import functools

import jax
import jax.numpy as jnp
from jax import lax
from jax.experimental import pallas as pl
from jax.experimental.pallas import tpu as pltpu

D_MODEL = 1024
DEPTH = 4
CONV_DIM = 256
POOL_DIM = 256
ATTN_DIM = 512
HEAD_DIM = 64
CONV_WIDTH = 3
POOL_WINDOWS = (2, 4, 8, 16)
POOL_GROUP_DIM = 64
MAX_POOL = 16
N_HEADS = 8
N_KV_HEADS = 2
GQA_GROUP = 4
WINDOW = 128
D_FF = 2816
KV_DIM = 128
D_IN = 1792
N_MOD = 9
ALPHA = (2.0 * DEPTH) ** 0.25
LN_EPS = 1e-5
RMS_EPS = 1e-6
NEG_INF = -1e30
ATTN_SCALE = HEAD_DIM ** -0.5

OFF_GB, OFF_GC, OFF_XIN, OFF_U, OFF_Q, OFF_K, OFF_V = 0, 256, 512, 768, 1024, 1536, 1664

LANES = 128
HALF_LANES = 64
SUBLANES = 8
VMEM_LIMIT_BYTES = 56 * 1024 * 1024

ADA_ROWS_PAD = 8
FFN_TOKENS = 512
FFN_CHUNK = 256
FFN_CHUNKS = D_FF // FFN_CHUNK
MIX_TOKENS = 512
CONV_PAD = 8
POOL_GUARD = SUBLANES
POOL_PAD = POOL_GUARD + MAX_POOL
SAMPLE_ATTN_BATCH = 16
PROJ_CHUNK = 512

F32 = jnp.float32
BF16 = jnp.bfloat16


def _const_spec(shape):
    zeros = (0,) * len(shape)
    return pl.BlockSpec(shape, lambda *_: zeros, pipeline_mode=pl.Buffered(1))


def _layer_spec(shape, l):
    zeros = (0,) * len(shape)
    return pl.BlockSpec((None,) + tuple(shape), lambda *_: (l,) + zeros,
                        pipeline_mode=pl.Buffered(1))


def _whole_out_spec(shape):
    zeros = (0,) * len(shape)
    return pl.BlockSpec(shape, lambda *_: zeros)


def _layer_norm(y, g, b):
    mu = jnp.mean(y, axis=-1, keepdims=True)
    d = y - mu
    var = jnp.mean(d * d, axis=-1, keepdims=True)
    return d * lax.rsqrt(var + LN_EPS) * g + b


def _rms_norm(y, g):
    return y * lax.rsqrt(jnp.mean(y * y, axis=-1, keepdims=True) + RMS_EPS) * g


def _low_half(shape):
    return (lax.broadcasted_iota(jnp.int32, shape, len(shape) - 1) % LANES) < HALF_LANES


def _pool_select(s2, s4, s8, s16):
    grp = lax.broadcasted_iota(jnp.int32, s16.shape, 1) // POOL_GROUP_DIM
    return jnp.where(grp == 0, s2, jnp.where(grp == 1, s4, jnp.where(grp == 2, s8, s16)))


def _ada_kernel(c_ref, w_ref, b_ref, op_ref, os_ref):
    c = c_ref[...]
    a = (c * jax.nn.sigmoid(c)).astype(BF16)
    r = jnp.dot(a, w_ref[...].astype(BF16), preferred_element_type=F32) + b_ref[...]
    n_prompt = op_ref.shape[0]
    op_ref[...] = r[:n_prompt]
    os_ref[...] = r[ADA_ROWS_PAD:]


def _ada(c_all, w_ada, b_ada, n_prompt, n_sample):
    rows = c_all.shape[0]
    return pl.pallas_call(
        _ada_kernel,
        grid=(DEPTH, N_MOD),
        in_specs=[
            pl.BlockSpec((rows, D_MODEL), lambda l, j: (0, 0)),
            pl.BlockSpec((None, D_MODEL, D_MODEL), lambda l, j: (l, 0, j)),
            pl.BlockSpec((None, 1, D_MODEL), lambda l, j: (l, 0, j)),
        ],
        out_specs=[
            pl.BlockSpec((None, None, n_prompt, D_MODEL), lambda l, j: (l, j, 0, 0)),
            pl.BlockSpec((None, None, n_sample, D_MODEL), lambda l, j: (l, j, 0, 0)),
        ],
        out_shape=[
            jax.ShapeDtypeStruct((DEPTH, N_MOD, n_prompt, D_MODEL), F32),
            jax.ShapeDtypeStruct((DEPTH, N_MOD, n_sample, D_MODEL), F32),
        ],
        compiler_params=pltpu.CompilerParams(
            dimension_semantics=("arbitrary", "arbitrary"), vmem_limit_bytes=VMEM_LIMIT_BYTES),
        name="ada",
    )(c_all, w_ada, b_ada.reshape(DEPTH, 1, N_MOD * D_MODEL))


def _ffn_kernel(n_tiles,
                xp_ref, mp_ref, xs_ref, ms_ref, wg_ref, wu_ref, wd_ref, lng_ref, lnb_ref,
                op_ref, os_ref, wg_bf, wu_bf, wd_bf):
    i = pl.program_id(0)

    for c in range(FFN_CHUNKS):
        @pl.when(i == c)
        def _(c=c):
            cols = slice(c * FFN_CHUNK, (c + 1) * FFN_CHUNK)
            wg_bf[:, cols] = wg_ref[...].astype(BF16)
            wu_bf[:, cols] = wu_ref[...].astype(BF16)
            wd_bf[cols, :] = wd_ref[...].astype(BF16)

    def ffn(x_ref, m_ref, o_ref):
        x = x_ref[...]
        shift, scale, gate = m_ref[0], m_ref[1], m_ref[2]
        h = (x * (1.0 + scale) + shift).astype(BF16)
        g = jnp.dot(h, wg_bf[...], preferred_element_type=F32)
        u = jnp.dot(h, wu_bf[...], preferred_element_type=F32)
        a = (g * jax.nn.sigmoid(g) * u).astype(BF16)
        sub = 0.5 * jnp.dot(a, wd_bf[...], preferred_element_type=F32)
        o_ref[...] = _layer_norm(ALPHA * x + gate * sub, lng_ref[...], lnb_ref[...])

    @pl.when((i >= FFN_CHUNKS) & (i < FFN_CHUNKS + n_tiles))
    def _():
        ffn(xp_ref, mp_ref, op_ref)

    @pl.when(i == FFN_CHUNKS + n_tiles)
    def _():
        ffn(xs_ref, ms_ref, os_ref)


def _ffn(xp, xs, mod_p, mod_s, l, sub, tiles_per_seq, wg, wu, wd, ln_g, ln_b):
    n, n_sample = xp.shape[0], xs.shape[0]
    tm = FFN_TOKENS
    n_tiles = n // tm

    def tile(i):
        return jnp.clip(i - FFN_CHUNKS, 0, n_tiles - 1)

    def chunk(i):
        return jnp.minimum(i, FFN_CHUNKS - 1)

    return pl.pallas_call(
        functools.partial(_ffn_kernel, n_tiles),
        grid=(FFN_CHUNKS + n_tiles + 1,),
        in_specs=[
            pl.BlockSpec((tm, D_MODEL), lambda i: (tile(i), 0)),
            pl.BlockSpec((None, 3, None, 1, D_MODEL),
                         lambda i: (l, sub, tile(i) // tiles_per_seq, 0, 0)),
            _const_spec((n_sample, D_MODEL)),
            pl.BlockSpec((None, 3, n_sample, D_MODEL), lambda i: (l, sub, 0, 0),
                         pipeline_mode=pl.Buffered(1)),
            pl.BlockSpec((None, D_MODEL, FFN_CHUNK), lambda i: (l, 0, chunk(i))),
            pl.BlockSpec((None, D_MODEL, FFN_CHUNK), lambda i: (l, 0, chunk(i))),
            pl.BlockSpec((None, FFN_CHUNK, D_MODEL), lambda i: (l, chunk(i), 0)),
            pl.BlockSpec((None, None, 1, D_MODEL), lambda i: (l, sub, 0, 0)),
            pl.BlockSpec((None, None, 1, D_MODEL), lambda i: (l, sub, 0, 0)),
        ],
        out_specs=[
            pl.BlockSpec((tm, D_MODEL), lambda i: (tile(i), 0)),
            _whole_out_spec((n_sample, D_MODEL)),
        ],
        out_shape=[
            jax.ShapeDtypeStruct((n, D_MODEL), F32),
            jax.ShapeDtypeStruct((n_sample, D_MODEL), F32),
        ],
        scratch_shapes=[
            pltpu.VMEM((D_MODEL, D_FF), BF16),
            pltpu.VMEM((D_MODEL, D_FF), BF16),
            pltpu.VMEM((D_FF, D_MODEL), BF16),
        ],
        compiler_params=pltpu.CompilerParams(
            dimension_semantics=("arbitrary",), vmem_limit_bytes=VMEM_LIMIT_BYTES),
        name="ffn",
    )(xp, mod_p, xs, mod_s, wg, wu, wd, ln_g, ln_b)


def _prompt_mixer_kernel(tiles_per_seq, n_tiles, layer,
                         sink_ref, xn_ref, mn_ref, xo_ref, mo_ref, win_ref, convw_ref, poolw_ref,
                         pscale_ref, mixg_ref, wout_ref, lng_ref, lnb_ref,
                         o_ref, oconv_ref, opool_ref, ok_ref, ov_ref,
                         win_bf, wout_bf, hn_buf, sub_buf, proj_a, proj_b, y_a, y_b,
                         cv_ext, u_ext, s2_ext, s4_ext, s8_ext, kd0, kd1, vd0, vd1, yattn):
    step = pl.program_id(0)
    tile = jnp.clip(step - 1, 0, n_tiles - 1) % tiles_per_seq

    @pl.when(step == 0)
    def _():
        win_bf[...] = win_ref[...].astype(BF16)
        wout_bf[...] = wout_ref[...].astype(BF16)
        proj_b[...] = jnp.zeros(proj_b.shape, F32)
        y_a[...] = jnp.zeros(y_a.shape, BF16)

    @pl.when(tile == 0)
    def _():
        cv_ext[0:CONV_PAD, :] = jnp.zeros((CONV_PAD, CONV_DIM), F32)
        u_ext[0:POOL_PAD, :] = jnp.zeros((POOL_PAD, POOL_DIM), F32)
        s2_ext[0:POOL_GUARD, :] = jnp.zeros((POOL_GUARD, POOL_DIM), F32)
        s4_ext[0:POOL_GUARD, :] = jnp.zeros((POOL_GUARD, POOL_DIM), F32)
        for ref in (kd0, kd1, vd0, vd1):
            ref[0:WINDOW, :] = jnp.zeros((WINDOW, KV_DIM), BF16)

    def stage(proj_next, proj, y_mix, y_out):
        _mixer_stage(layer, tile, proj_next, proj, y_mix, y_out,
                     sink_ref, xn_ref, mn_ref, xo_ref, mo_ref, convw_ref, poolw_ref, pscale_ref,
                     mixg_ref, win_bf, wout_bf, lng_ref, lnb_ref,
                     o_ref, oconv_ref, opool_ref, ok_ref, ov_ref,
                     hn_buf, sub_buf, cv_ext, u_ext, s2_ext, s4_ext, s8_ext,
                     kd0, kd1, vd0, vd1, yattn)

    @pl.when(step % 2 == 0)
    def _():
        stage(proj_a, proj_b, y_b, y_a)

    @pl.when(step % 2 == 1)
    def _():
        stage(proj_b, proj_a, y_a, y_b)


def _mixer_stage(layer, tile, proj_next, proj, y_mix, y_out,
                 sink_ref, xn_ref, mn_ref, xo_ref, mo_ref, convw_ref, poolw_ref, pscale_ref,
                 mixg_ref, win_bf, wout_bf, lng_ref, lnb_ref,
                 o_ref, oconv_ref, opool_ref, ok_ref, ov_ref,
                 hn_buf, sub_buf, cv_ext, u_ext, s2_ext, s4_ext, s8_ext,
                 kd0, kd1, vd0, vd1, yattn):
    tm = xn_ref.shape[0]
    mixg = mixg_ref[...]

    def project(c):
        cols = slice(c * PROJ_CHUNK, min((c + 1) * PROJ_CHUNK, D_IN))
        proj_next[:, cols] = jnp.dot(hn_buf[...], win_bf[:, cols], preferred_element_type=F32)

    def out_project(c):
        cols = slice(c * PROJ_CHUNK, (c + 1) * PROJ_CHUNK)
        sub_buf[:, cols] = jnp.dot(y_out[...], wout_bf[:, cols], preferred_element_type=F32)

    n_proj = -(-D_IN // PROJ_CHUNK)
    hn_buf[...] = (xn_ref[...] * (1.0 + mn_ref[1]) + mn_ref[0]).astype(BF16)
    out_project(0)

    cv = proj[:, OFF_GC:OFF_GC + CONV_DIM] * proj[:, OFF_XIN:OFF_XIN + CONV_DIM]
    cv_ext[CONV_PAD:CONV_PAD + tm, :] = cv
    conv = convw_ref[CONV_WIDTH - 1:CONV_WIDTH, :] * cv
    for kk in range(CONV_WIDTH - 1):
        back = CONV_WIDTH - 1 - kk
        conv = conv + convw_ref[kk:kk + 1, :] * cv_ext[CONV_PAD - back:CONV_PAD - back + tm, :]
    y_conv = proj[:, OFF_GB:OFF_GB + CONV_DIM] * conv
    y_mix[:, 0:CONV_DIM] = _rms_norm(y_conv, mixg[:, 0:CONV_DIM]).astype(BF16)
    oconv_ref[...] = cv_ext[CONV_PAD + tm - (CONV_WIDTH - 1):CONV_PAD + tm, :]
    cv_ext[0:CONV_PAD, :] = cv_ext[tm:tm + CONV_PAD, :]
    out_project(1)

    u = proj[:, OFF_U:OFF_U + POOL_DIM]
    u_ext[POOL_PAD:POOL_PAD + tm, :] = u
    lo, n_ext = POOL_GUARD, POOL_PAD - POOL_GUARD + tm
    sums = {1: u_ext}
    for w, dst in ((1, s2_ext), (2, s4_ext), (4, s8_ext)):
        src = sums[w]
        dst[lo:lo + n_ext, :] = src[lo:lo + n_ext, :] + src[lo - w:lo - w + n_ext, :]
        sums[2 * w] = dst
    half = MAX_POOL // 2
    s16 = s8_ext[POOL_PAD:POOL_PAD + tm, :] + s8_ext[POOL_PAD - half:POOL_PAD - half + tm, :]
    sums = {w: sums[w][POOL_PAD:POOL_PAD + tm, :] for w in (2, 4, 8)}
    sums[16] = s16
    pos1 = tile * tm + lax.broadcasted_iota(jnp.int32, (tm, 1), 0) + 1
    counts = [jnp.minimum(pos1, w).astype(F32) for w in POOL_WINDOWS]
    win_sum = _pool_select(*[sums[w] for w in POOL_WINDOWS])
    count = _pool_select(*[jnp.broadcast_to(c, (tm, POOL_DIM)) for c in counts])
    pooled = (win_sum / count - u).astype(BF16)
    y_pool = jnp.dot(pooled, poolw_ref[...], preferred_element_type=F32) * pscale_ref[...]
    y_mix[:, CONV_DIM:CONV_DIM + POOL_DIM] = _rms_norm(
        y_pool, mixg[:, CONV_DIM:CONV_DIM + POOL_DIM]).astype(BF16)
    opool_ref[...] = u_ext[POOL_PAD + tm - (MAX_POOL - 1):POOL_PAD + tm, :]
    u_ext[POOL_GUARD:POOL_PAD, :] = u_ext[tm + POOL_GUARD:tm + POOL_PAD, :]

    k = proj[:, OFF_K:OFF_K + KV_DIM]
    v = proj[:, OFF_V:OFF_V + KV_DIM]
    ok_ref[...] = k[tm - WINDOW:, :]
    ov_ref[...] = v[tm - WINDOW:, :]
    low = _low_half((tm, KV_DIM))
    k_sw = pltpu.roll(k, HALF_LANES, 1)
    v_sw = pltpu.roll(v, HALF_LANES, 1)
    kd0[WINDOW:WINDOW + tm, :] = jnp.where(low, k, k_sw).astype(BF16)
    kd1[WINDOW:WINDOW + tm, :] = jnp.where(low, k_sw, k).astype(BF16)
    vd0[WINDOW:WINDOW + tm, :] = jnp.where(low, v, v_sw).astype(BF16)
    vd1[WINDOW:WINDOW + tm, :] = jnp.where(low, v_sw, v).astype(BF16)

    rows = GQA_GROUP * WINDOW
    r = lax.broadcasted_iota(jnp.int32, (rows, 2 * WINDOW), 0)
    qidx = r % WINDOW
    kidx = lax.broadcasted_iota(jnp.int32, (rows, 2 * WINDOW), 1)
    band = (kidx > qidx) & (kidx <= qidx + WINDOW)
    first_key = jnp.where(tile == 0, WINDOW, 0)
    head_row = lax.broadcasted_iota(jnp.int32, (rows, 1), 0) // WINDOW
    low_q = _low_half((WINDOW, LANES))
    next_chunk = 0
    for g, (kd, vd) in enumerate(((kd0, vd0), (kd1, vd1))):
        sink = jnp.zeros((rows, 1), F32)
        for hh in range(GQA_GROUP):
            sink = jnp.where(head_row == hh, sink_ref[layer, g * GQA_GROUP + hh], sink)
        for j in range(tm // WINDOW):
            mask = (band & (kidx >= first_key)) if j == 0 else band
            q0 = OFF_Q + g * GQA_GROUP * HEAD_DIM
            qa = proj[j * WINDOW:(j + 1) * WINDOW, q0:q0 + LANES] * ATTN_SCALE
            qb = proj[j * WINDOW:(j + 1) * WINDOW, q0 + LANES:q0 + 2 * LANES] * ATTN_SCALE
            q4 = jnp.concatenate([jnp.where(low_q, qa, 0.0), jnp.where(low_q, 0.0, qa),
                                  jnp.where(low_q, qb, 0.0), jnp.where(low_q, 0.0, qb)],
                                 axis=0).astype(BF16)
            kk = kd[j * WINDOW:(j + 2) * WINDOW, :]
            vv = vd[j * WINDOW:(j + 2) * WINDOW, :]
            s = lax.dot_general(q4, kk, (((1,), (1,)), ((), ())), preferred_element_type=F32)
            s = jnp.where(mask, s, NEG_INF)
            mx = jnp.maximum(jnp.max(s, axis=-1, keepdims=True), sink)
            p = jnp.exp(s - mx)
            den = jnp.sum(p, axis=-1, keepdims=True) + jnp.exp(sink - mx)
            o = jnp.dot(p.astype(BF16), vv, preferred_element_type=F32) / den
            ya = jnp.where(low_q, o[0:WINDOW], o[WINDOW:2 * WINDOW])
            yb = jnp.where(low_q, o[2 * WINDOW:3 * WINDOW], o[3 * WINDOW:4 * WINDOW])
            c0 = g * GQA_GROUP * HEAD_DIM
            yattn[j * WINDOW:(j + 1) * WINDOW, c0:c0 + LANES] = ya
            yattn[j * WINDOW:(j + 1) * WINDOW, c0 + LANES:c0 + 2 * LANES] = yb
            if j % 2 == 1 and next_chunk < n_proj:
                project(next_chunk)
                next_chunk += 1
    for ref in (kd0, kd1, vd0, vd1):
        ref[0:WINDOW, :] = ref[tm:tm + WINDOW, :]

    o_ref[...] = _layer_norm(ALPHA * xo_ref[...] + mo_ref[2] * sub_buf[...],
                             lng_ref[...], lnb_ref[...])
    while next_chunk < n_proj:
        project(next_chunk)
        next_chunk += 1
    y_mix[:, CONV_DIM + POOL_DIM:] = _rms_norm(
        yattn[...], mixg[:, CONV_DIM + POOL_DIM:]).astype(BF16)


def _prompt_mixer(x, mod, l, sinks, w_in, conv_w, pool_wbd, pool_scale, mix_g, w_out,
                  ln_g, ln_b, batch, seq):
    tm = MIX_TOKENS
    tiles_per_seq = seq // tm
    n = x.shape[0]
    n_tiles = n // tm

    def nxt(i):
        return jnp.minimum(i, n_tiles - 1)

    def cur(i):
        return jnp.clip(i - 1, 0, n_tiles - 1)

    def fin(i):
        return jnp.clip(i - 2, 0, n_tiles - 1)

    return pl.pallas_call(
        functools.partial(_prompt_mixer_kernel, tiles_per_seq, n_tiles, l),
        grid=(n_tiles + 2,),
        in_specs=[
            pl.BlockSpec(memory_space=pltpu.SMEM),
            pl.BlockSpec((tm, D_MODEL), lambda i: (nxt(i), 0)),
            pl.BlockSpec((None, 3, None, 1, D_MODEL),
                         lambda i: (l, 1, nxt(i) // tiles_per_seq, 0, 0)),
            pl.BlockSpec((tm, D_MODEL), lambda i: (fin(i), 0)),
            pl.BlockSpec((None, 3, None, 1, D_MODEL),
                         lambda i: (l, 1, fin(i) // tiles_per_seq, 0, 0)),
            _layer_spec((D_MODEL, D_IN), l),
            _layer_spec((CONV_WIDTH, CONV_DIM), l),
            _layer_spec((POOL_DIM, POOL_DIM), l),
            _layer_spec((1, POOL_DIM), l),
            _layer_spec((1, D_MODEL), l),
            _layer_spec((D_MODEL, D_MODEL), l),
            pl.BlockSpec((None, None, 1, D_MODEL), lambda i: (l, 1, 0, 0)),
            pl.BlockSpec((None, None, 1, D_MODEL), lambda i: (l, 1, 0, 0)),
        ],
        out_specs=[
            pl.BlockSpec((tm, D_MODEL), lambda i: (fin(i), 0)),
            pl.BlockSpec((None, CONV_WIDTH - 1, CONV_DIM),
                         lambda i: (cur(i) // tiles_per_seq, 0, 0)),
            pl.BlockSpec((None, MAX_POOL - 1, POOL_DIM),
                         lambda i: (cur(i) // tiles_per_seq, 0, 0)),
            pl.BlockSpec((None, WINDOW, KV_DIM), lambda i: (cur(i) // tiles_per_seq, 0, 0)),
            pl.BlockSpec((None, WINDOW, KV_DIM), lambda i: (cur(i) // tiles_per_seq, 0, 0)),
        ],
        out_shape=[
            jax.ShapeDtypeStruct((n, D_MODEL), F32),
            jax.ShapeDtypeStruct((batch, CONV_WIDTH - 1, CONV_DIM), F32),
            jax.ShapeDtypeStruct((batch, MAX_POOL - 1, POOL_DIM), F32),
            jax.ShapeDtypeStruct((batch, WINDOW, KV_DIM), F32),
            jax.ShapeDtypeStruct((batch, WINDOW, KV_DIM), F32),
        ],
        scratch_shapes=[
            pltpu.VMEM((D_MODEL, D_IN), BF16),
            pltpu.VMEM((D_MODEL, D_MODEL), BF16),
            pltpu.VMEM((tm, D_MODEL), BF16),
            pltpu.VMEM((tm, D_MODEL), F32),
            pltpu.VMEM((tm, D_IN), F32),
            pltpu.VMEM((tm, D_IN), F32),
            pltpu.VMEM((tm, D_MODEL), BF16),
            pltpu.VMEM((tm, D_MODEL), BF16),
            pltpu.VMEM((CONV_PAD + tm, CONV_DIM), F32),
            pltpu.VMEM((POOL_PAD + tm, POOL_DIM), F32),
            pltpu.VMEM((POOL_PAD + tm, POOL_DIM), F32),
            pltpu.VMEM((POOL_PAD + tm, POOL_DIM), F32),
            pltpu.VMEM((POOL_PAD + tm, POOL_DIM), F32),
            pltpu.VMEM((WINDOW + tm, KV_DIM), BF16),
            pltpu.VMEM((WINDOW + tm, KV_DIM), BF16),
            pltpu.VMEM((WINDOW + tm, KV_DIM), BF16),
            pltpu.VMEM((WINDOW + tm, KV_DIM), BF16),
            pltpu.VMEM((tm, ATTN_DIM), F32),
        ],
        compiler_params=pltpu.CompilerParams(
            dimension_semantics=("arbitrary",), vmem_limit_bytes=VMEM_LIMIT_BYTES),
        name="prompt_mixer",
    )(sinks, x, mod, x, mod, w_in, conv_w, pool_wbd, pool_scale, mix_g, w_out, ln_g, ln_b)


def _sample_proj_kernel(x_ref, m_ref, win_ref, convw_ref, poolw_ref, pscale_ref, mixg_ref,
                        sconv_ref, spool_ref,
                        qm_ref, knew_ref, vnew_ref, ycp_ref, nconv_ref, npool_ref):
    x = x_ref[...]
    shift, scale = m_ref[0], m_ref[1]
    h = (x * (1.0 + scale) + shift).astype(BF16)
    proj = jnp.dot(h, win_ref[...].astype(BF16), preferred_element_type=F32)

    cv = proj[:, OFF_GC:OFF_GC + CONV_DIM] * proj[:, OFF_XIN:OFF_XIN + CONV_DIM]
    conv = convw_ref[CONV_WIDTH - 1:CONV_WIDTH, :] * cv
    for kk in range(CONV_WIDTH - 1):
        conv = conv + convw_ref[kk:kk + 1, :] * sconv_ref[:, kk * CONV_DIM:(kk + 1) * CONV_DIM]
    y_conv = proj[:, OFF_GB:OFF_GB + CONV_DIM] * conv
    keep = (CONV_WIDTH - 2) * CONV_DIM
    nconv_ref[:, 0:keep] = sconv_ref[:, CONV_DIM:CONV_DIM + keep]
    nconv_ref[:, keep:keep + CONV_DIM] = cv

    u = proj[:, OFF_U:OFF_U + POOL_DIM]
    acc = u
    sums = {}
    for back in range(1, MAX_POOL):
        acc = acc + spool_ref[MAX_POOL - 1 - back]
        if back + 1 in POOL_WINDOWS:
            sums[back + 1] = acc
    win_sum = _pool_select(*[sums[w] for w in POOL_WINDOWS])
    count = _pool_select(*[jnp.full(u.shape, float(w), F32) for w in POOL_WINDOWS])
    pooled = (win_sum / count - u).astype(BF16)
    y_pool = jnp.dot(pooled, poolw_ref[...], preferred_element_type=F32) * pscale_ref[...]
    npool_ref[0:MAX_POOL - 2] = spool_ref[1:MAX_POOL - 1]
    npool_ref[MAX_POOL - 2] = u

    mixg = mixg_ref[...]
    ycp_ref[:, 0:CONV_DIM] = _rms_norm(y_conv, mixg[:, 0:CONV_DIM])
    ycp_ref[:, CONV_DIM:CONV_DIM + POOL_DIM] = _rms_norm(y_pool, mixg[:, CONV_DIM:CONV_DIM + POOL_DIM])

    knew_ref[...] = proj[:, OFF_K:OFF_K + KV_DIM].T
    vnew_ref[...] = proj[:, OFF_V:OFF_V + KV_DIM].T
    low = _low_half((x.shape[0], LANES))
    for pair in range(N_HEADS // 2):
        g = (2 * pair) // GQA_GROUP
        qp = proj[:, OFF_Q + pair * LANES:OFF_Q + (pair + 1) * LANES]
        qs = pltpu.roll(qp, HALF_LANES, 1)
        if g == 0:
            first, second = jnp.where(low, qp, 0.0), jnp.where(low, qs, 0.0)
        else:
            first, second = jnp.where(low, 0.0, qs), jnp.where(low, 0.0, qp)
        qm_ref[:, (2 * pair) * LANES:(2 * pair + 1) * LANES] = first
        qm_ref[:, (2 * pair + 1) * LANES:(2 * pair + 2) * LANES] = second


def _sample_proj(x, mod_s, l, w_in, conv_w, pool_wbd, pool_scale, mix_g, sconv, spool):
    n = x.shape[0]
    return pl.pallas_call(
        _sample_proj_kernel,
        grid=(1,),
        in_specs=[
            _const_spec((n, D_MODEL)),
            pl.BlockSpec((None, 3, n, D_MODEL), lambda i: (l, 1, 0, 0)),
            _layer_spec((D_MODEL, D_IN), l),
            _layer_spec((CONV_WIDTH, CONV_DIM), l),
            _layer_spec((POOL_DIM, POOL_DIM), l),
            _layer_spec((1, POOL_DIM), l),
            _layer_spec((1, D_MODEL), l),
            _layer_spec((n, (CONV_WIDTH - 1) * CONV_DIM), l),
            _layer_spec((MAX_POOL - 1, n, POOL_DIM), l),
        ],
        out_specs=[
            _whole_out_spec((n, N_HEADS * LANES)),
            _whole_out_spec((KV_DIM, n)),
            _whole_out_spec((KV_DIM, n)),
            _whole_out_spec((n, CONV_DIM + POOL_DIM)),
            _whole_out_spec((n, (CONV_WIDTH - 1) * CONV_DIM)),
            _whole_out_spec((MAX_POOL - 1, n, POOL_DIM)),
        ],
        out_shape=[
            jax.ShapeDtypeStruct((n, N_HEADS * LANES), F32),
            jax.ShapeDtypeStruct((KV_DIM, n), F32),
            jax.ShapeDtypeStruct((KV_DIM, n), F32),
            jax.ShapeDtypeStruct((n, CONV_DIM + POOL_DIM), F32),
            jax.ShapeDtypeStruct((n, (CONV_WIDTH - 1) * CONV_DIM), F32),
            jax.ShapeDtypeStruct((MAX_POOL - 1, n, POOL_DIM), F32),
        ],
        compiler_params=pltpu.CompilerParams(
            dimension_semantics=("arbitrary",), vmem_limit_bytes=VMEM_LIMIT_BYTES),
        name="sample_proj",
    )(x, mod_s, w_in, conv_w, pool_wbd, pool_scale, mix_g, sconv, spool)


def _sample_attn_kernel(qm_ref, knew_ref, vnew_ref, ck_ref, cv_ref, sink_ref,
                        nk_ref, nv_ref, o_ref):
    bb = qm_ref.shape[0]
    first_seq = pl.program_id(0) * bb
    newest = lax.broadcasted_iota(jnp.int32, (KV_DIM, WINDOW), 1) == WINDOW - 1
    seq_lane = lax.broadcasted_iota(jnp.int32, knew_ref.shape, 1)
    for b in range(bb):
        mine = seq_lane == first_seq + b
        for new_ref, old_ref, out_ref in ((knew_ref, ck_ref, nk_ref), (vnew_ref, cv_ref, nv_ref)):
            col = jnp.sum(jnp.where(mine, new_ref[...], 0.0), axis=1, keepdims=True)
            out_ref[b] = jnp.where(newest, col, pltpu.roll(old_ref[b], WINDOW - 1, 1))
    q = qm_ref[...].astype(BF16)
    s = jnp.einsum('bhc,bck->bhk', q, nk_ref[...].astype(BF16),
                   preferred_element_type=F32) * ATTN_SCALE
    sink = sink_ref[...][None]
    mx = jnp.maximum(jnp.max(s, axis=-1, keepdims=True), sink)
    p = jnp.exp(s - mx)
    den = jnp.sum(p, axis=-1, keepdims=True) + jnp.exp(sink - mx)
    o = jnp.einsum('bhk,bck->bhc', p.astype(BF16), nv_ref[...].astype(BF16),
                   preferred_element_type=F32)
    o_ref[...] = o / den


def _sample_attn(qm, knew_t, vnew_t, cache_k, cache_v, l, sinks):
    n = qm.shape[0]
    bb = SAMPLE_ATTN_BATCH
    cache = pl.BlockSpec((None, bb, KV_DIM, WINDOW), lambda i: (l, i, 0, 0))
    win = pl.BlockSpec((bb, KV_DIM, WINDOW), lambda i: (i, 0, 0))
    heads = pl.BlockSpec((bb, N_HEADS, LANES), lambda i: (i, 0, 0))
    return pl.pallas_call(
        _sample_attn_kernel,
        grid=(n // bb,),
        in_specs=[heads, _const_spec((KV_DIM, n)), _const_spec((KV_DIM, n)), cache, cache,
                  _layer_spec((N_HEADS, 1), l)],
        out_specs=[win, win, heads],
        out_shape=[
            jax.ShapeDtypeStruct((n, KV_DIM, WINDOW), F32),
            jax.ShapeDtypeStruct((n, KV_DIM, WINDOW), F32),
            jax.ShapeDtypeStruct((n, N_HEADS, LANES), F32),
        ],
        compiler_params=pltpu.CompilerParams(
            dimension_semantics=("arbitrary",), vmem_limit_bytes=VMEM_LIMIT_BYTES),
        name="sample_attn",
    )(qm, knew_t, vnew_t, cache_k, cache_v, sinks)


def _sample_out_kernel(x_ref, m_ref, ycp_ref, oh_ref, mixg_ref, wout_ref, lng_ref, lnb_ref, o_ref):
    x = x_ref[...]
    gate = m_ref[2]
    low = _low_half((x.shape[0], LANES))
    pairs = []
    for pair in range(N_HEADS // 2):
        g = (2 * pair) // GQA_GROUP
        oa = oh_ref[:, (2 * pair) * LANES:(2 * pair + 1) * LANES]
        ob = oh_ref[:, (2 * pair + 1) * LANES:(2 * pair + 2) * LANES]
        if g == 0:
            pairs.append(jnp.where(low, oa, pltpu.roll(ob, HALF_LANES, 1)))
        else:
            pairs.append(jnp.where(low, pltpu.roll(oa, HALF_LANES, 1), ob))
    y_attn = jnp.concatenate(pairs, axis=-1)
    mixg = mixg_ref[...]
    y = jnp.concatenate([ycp_ref[...], _rms_norm(y_attn, mixg[:, CONV_DIM + POOL_DIM:])],
                        axis=-1).astype(BF16)
    sub = jnp.dot(y, wout_ref[...].astype(BF16), preferred_element_type=F32)
    o_ref[...] = _layer_norm(ALPHA * x + gate * sub, lng_ref[...], lnb_ref[...])


def _sample_out(x, mod_s, l, ycp, oh, mix_g, w_out, ln_g, ln_b):
    n = x.shape[0]
    return pl.pallas_call(
        _sample_out_kernel,
        grid=(1,),
        in_specs=[
            _const_spec((n, D_MODEL)),
            pl.BlockSpec((None, 3, n, D_MODEL), lambda i: (l, 1, 0, 0)),
            _const_spec((n, CONV_DIM + POOL_DIM)),
            _const_spec((n, N_HEADS * LANES)),
            _layer_spec((1, D_MODEL), l),
            _layer_spec((D_MODEL, D_MODEL), l),
            pl.BlockSpec((None, None, 1, D_MODEL), lambda i: (l, 1, 0, 0)),
            pl.BlockSpec((None, None, 1, D_MODEL), lambda i: (l, 1, 0, 0)),
        ],
        out_specs=_whole_out_spec((n, D_MODEL)),
        out_shape=jax.ShapeDtypeStruct((n, D_MODEL), F32),
        compiler_params=pltpu.CompilerParams(
            dimension_semantics=("arbitrary",), vmem_limit_bytes=VMEM_LIMIT_BYTES),
        name="sample_out",
    )(x, mod_s, ycp, oh, mix_g, w_out, ln_g, ln_b)


def _block_diag(w):
    n, d, _ = w.shape
    eye = jnp.eye(n, dtype=w.dtype)
    return (eye[:, None, :, None] * w[:, :, None, :]).reshape(n * d, n * d)


def kernel(x_prompt, x_sample, state_conv, state_pool, cache_k_win, cache_v_win, c_prompt, c_sample, ln_g, ln_b, w_ada, b_ada, ffn1_gate, ffn1_up, ffn1_down, w_in, conv_w, pool_w, pool_scale, attn_sinks, mix_norm_g, w_out, ffn2_gate, ffn2_up, ffn2_down):
    batch, seq, _ = x_prompt.shape
    n_sample = x_sample.shape[0]
    win_buf = cache_k_win.shape[2]
    assert x_sample.shape[1] == 1 and win_buf == WINDOW
    assert seq % MIX_TOKENS == 0 and (batch * seq) % FFN_TOKENS == 0 and seq % FFN_TOKENS == 0
    assert n_sample % SAMPLE_ATTN_BATCH == 0 and batch <= ADA_ROWS_PAD

    c_all = jnp.concatenate(
        [c_prompt, jnp.zeros((ADA_ROWS_PAD - batch, D_MODEL), F32), c_sample], axis=0)
    mod_p, mod_s = _ada(c_all, w_ada, b_ada, batch, n_sample)
    mod_p = mod_p.reshape(DEPTH, N_MOD, batch, 1, D_MODEL)

    xp = x_prompt.reshape(batch * seq, D_MODEL)
    xs = x_sample.reshape(n_sample, D_MODEL)
    sconv = state_conv.reshape(DEPTH, n_sample, (CONV_WIDTH - 1) * CONV_DIM)
    spool = jnp.transpose(state_pool, (0, 2, 1, 3))
    cache_k = jnp.transpose(cache_k_win, (0, 1, 3, 4, 2)).reshape(DEPTH, n_sample, KV_DIM, WINDOW)
    cache_v = jnp.transpose(cache_v_win, (0, 1, 3, 4, 2)).reshape(DEPTH, n_sample, KV_DIM, WINDOW)
    pool_wbd = jax.vmap(_block_diag)(pool_w).astype(BF16)
    pscale = pool_scale.reshape(DEPTH, 1, POOL_DIM)
    mixg = mix_norm_g.reshape(DEPTH, 1, D_MODEL)
    lng = ln_g.reshape(DEPTH, 3, 1, D_MODEL)
    lnb = ln_b.reshape(DEPTH, 3, 1, D_MODEL)
    sinks_col = attn_sinks.reshape(DEPTH, N_HEADS, 1)

    ffn_tiles_per_seq = seq // FFN_TOKENS
    outs = {k: [] for k in ("pc", "pp", "pk", "pv", "sc", "sp", "sk", "sv")}
    for l in range(DEPTH):
        xp, xs = _ffn(xp, xs, mod_p, mod_s, l, 0, ffn_tiles_per_seq,
                      ffn1_gate, ffn1_up, ffn1_down, lng, lnb)

        xp, pc, pp, pk, pv = _prompt_mixer(
            xp, mod_p, l, attn_sinks, w_in, conv_w, pool_wbd, pscale, mixg, w_out,
            lng, lnb, batch, seq)

        qm, knew, vnew, ycp, nconv, npool = _sample_proj(
            xs, mod_s, l, w_in, conv_w, pool_wbd, pscale, mixg, sconv, spool)
        nk, nv, oh = _sample_attn(
            qm.reshape(n_sample, N_HEADS, LANES), knew, vnew, cache_k, cache_v, l, sinks_col)
        xs = _sample_out(xs, mod_s, l, ycp, oh.reshape(n_sample, N_HEADS * LANES), mixg, w_out,
                         lng, lnb)

        xp, xs = _ffn(xp, xs, mod_p, mod_s, l, 2, ffn_tiles_per_seq,
                      ffn2_gate, ffn2_up, ffn2_down, lng, lnb)

        for key, val in zip(("pc", "pp", "pk", "pv", "sc", "sp", "sk", "sv"),
                            (pc, pp, pk, pv, nconv, npool, nk, nv)):
            outs[key].append(val)

    kv_shape = (DEPTH, -1, WINDOW, N_KV_HEADS, HEAD_DIM)
    kv_t_shape = (DEPTH, -1, N_KV_HEADS, HEAD_DIM, WINDOW)
    return (
        xp.reshape(batch, seq, D_MODEL),
        xs.reshape(n_sample, 1, D_MODEL),
        jnp.stack(outs["pc"]),
        jnp.stack(outs["pp"]),
        jnp.stack(outs["pk"]).reshape(kv_shape),
        jnp.stack(outs["pv"]).reshape(kv_shape),
        jnp.stack(outs["sc"]).reshape(DEPTH, n_sample, CONV_WIDTH - 1, CONV_DIM),
        jnp.transpose(jnp.stack(outs["sp"]), (0, 2, 1, 3)),
        jnp.transpose(jnp.stack(outs["sk"]).reshape(kv_t_shape), (0, 1, 4, 2, 3)),
        jnp.transpose(jnp.stack(outs["sv"]).reshape(kv_t_shape), (0, 1, 4, 2, 3)),
    )
```
